```python
import math
import jax, jax.numpy as jnp
from jax import lax
import numpy as np

D_MODEL = 1024
BATCH = 4
SEQ = 4096
DEPTH = 4
DEC_BATCH = 32
DEC_SEQ = 1
PAST_LEN = 8192
PAGE_SIZE = 128

N_MIXERS = 2
N_CONV_LAYERS = (DEPTH + 1) // 2
N_ATTN_LAYERS = DEPTH // 2
SC_WIDTH = 3
GROUPS = ((128, 1), (512, 4), (2048, 16))
N_GROUPS = len(GROUPS)
H_SLOT = 8
HEAD_DIM = 64
ATTN_WIDTH = H_SLOT * HEAD_DIM
QKV_WIDTH = N_GROUPS * 3 * ATTN_WIDTH
N_BUCKETS = 32
MAX_DISTANCE = 2048
QB = 128
D_FF = 2816
FFN_CONV_WIDTH = 3
EPS = 1e-6
SCALE = HEAD_DIM ** -0.5
NEG = -1e30

kernel_name = 'hybrid_shortconv_dilated_swa_convffn_step'


def rmsnorm(x, g):
    xf = x.astype(jnp.float32)
    r = lax.rsqrt(jnp.mean(xf * xf, axis=-1, keepdims=True) + EPS)
    return (xf * r).astype(x.dtype) * g


def causal_dwconv(u, hist, w):
    width = w.shape[0]
    t = u.shape[1]
    uh = jnp.concatenate([hist.astype(u.dtype), u], axis=1)
    out = w[0] * uh[:, 0:t]
    for j in range(1, width):
        out = out + w[j] * uh[:, j:j + t]
    return out, uh[:, uh.shape[1] - (width - 1):]


def shortconv_mixer(h, hist, w_in, conv_w, w_out):
    bg, cg, xv = jnp.split(h @ w_in, 3, axis=-1)
    conv, new_hist = causal_dwconv(cg * xv, hist, conv_w)
    return (bg * conv) @ w_out, new_hist


def convffn(h, hist, w_gate, w_up, conv_w, conv_b, w_down):
    gconv, new_hist = causal_dwconv(h @ w_gate, hist, conv_w)
    return (jax.nn.silu(gconv + conv_b) * (h @ w_up)) @ w_down, new_hist


def project_qkv(h, w_qkv, q_norm, k_norm):
    b, t, _ = h.shape
    p = (h @ w_qkv).reshape(b, t, N_GROUPS, 3, H_SLOT, HEAD_DIM)
    q = rmsnorm(p[:, :, :, 0], q_norm)
    k = rmsnorm(p[:, :, :, 1], k_norm)
    v = p[:, :, :, 2]
    return q, k, v


def _t5_bucket(dist):
    exact = N_BUCKETS // 2
    n = np.asarray(dist, dtype=np.float32)
    large = exact + np.log(np.maximum(n, 1.0) / exact) / math.log(MAX_DISTANCE / exact) * (N_BUCKETS - exact)
    large = np.minimum(np.floor(large), N_BUCKETS - 1)
    return np.where(n < exact, n, large).astype(np.int32)


def _group_biases(rel_bias):
    out = []
    for g, (win, dil) in enumerate(GROUPS):
        bucket = _t5_bucket(dil * np.arange(win // dil + 1))
        out.append(jnp.take(rel_bias, bucket, axis=0)[:, g * H_SLOT:(g + 1) * H_SLOT].T)
    return out


def dilated_branch(q, span_k, span_v, idx, valid, bias):
    idx_c = np.maximum(idx, 0).astype(np.int32)
    kg = jnp.take(span_k, idx_c, axis=1)
    vg = jnp.take(span_v, idx_c, axis=1)
    logits = jnp.einsum('bqhd,bqkhd->bqhk', q, kg, preferred_element_type=jnp.float32) * SCALE
    logits = logits + bias.astype(jnp.float32)[None, None]
    logits = jnp.where(valid[None, :, None, :], logits, NEG)
    m = jnp.max(logits, axis=-1, keepdims=True)
    e = jnp.exp(logits - m)
    s = jnp.sum(e, axis=-1, keepdims=True)
    out = jnp.einsum('bqhk,bqkhd->bqhd', (e / s).astype(span_v.dtype), vg)
    lse = (m + jnp.log(s))[..., 0]
    return out, lse


def merge_groups(outs, lses):
    w = jax.nn.softmax(jnp.stack(lses, axis=0), axis=0)
    out = w[0][..., None].astype(outs[0].dtype) * outs[0]
    for g in range(1, len(outs)):
        out = out + w[g][..., None].astype(outs[g].dtype) * outs[g]
    return out


def dilated_attention_prompt(q, k, v, biases):
    b, t = q.shape[0], q.shape[1]
    pads = []
    for g, (win, dil) in enumerate(GROUPS):
        pw = ((0, 0), (win, 0), (0, 0), (0, 0))
        pads.append((jnp.pad(k[:, :, g], pw), jnp.pad(v[:, :, g], pw)))
    qi = np.arange(QB, dtype=np.int32)[:, None]

    def block(t0):
        qb = lax.dynamic_slice_in_dim(q, t0, QB, axis=1)
        outs, lses = [], []
        for g, (win, dil) in enumerate(GROUPS):
            back = dil * np.arange(win // dil + 1, dtype=np.int32)[None, :]
            span_k = lax.dynamic_slice_in_dim(pads[g][0], t0, win + QB, axis=1)
            span_v = lax.dynamic_slice_in_dim(pads[g][1], t0, win + QB, axis=1)
            valid = (t0 + jnp.asarray(qi - back, dtype=jnp.int32)) >= 0
            o, l = dilated_branch(qb[:, :, g], span_k, span_v, qi + win - back, valid, biases[g])
            outs.append(o)
            lses.append(l)
        return merge_groups(outs, lses)

    out = lax.map(block, jnp.arange(t // QB, dtype=jnp.int32) * QB)
    out = jnp.moveaxis(out, 0, 1).reshape(b, t, ATTN_WIDTH)
    new_cache = []
    for g, (win, dil) in enumerate(GROUPS):
        keep = min(win, t)
        new_cache.append(jnp.stack([k[:, t - keep:, g], v[:, t - keep:, g]], axis=2))
    return out, new_cache


def dilated_attention_sample(q, k, v, caches, biases):
    b, s = q.shape[0], q.shape[1]
    qi = np.arange(s, dtype=np.int32)[:, None]
    outs, lses, new_cache = [], [], []
    for g, (win, dil) in enumerate(GROUPS):
        cache = caches[g]
        keep = cache.shape[1]
        back = dil * np.arange(win // dil + 1, dtype=np.int32)[None, :]
        span_k = jnp.concatenate([cache[:, :, 0], k[:, :, g]], axis=1)
        span_v = jnp.concatenate([cache[:, :, 1], v[:, :, g]], axis=1)
        idx = keep + qi - back
        o, l = dilated_branch(q[:, :, g], span_k, span_v, idx, jnp.asarray(idx >= 0), biases[g])
        outs.append(o)
        lses.append(l)
        n = span_k.shape[1]
        new_cache.append(jnp.stack([span_k[:, n - keep:], span_v[:, n - keep:]], axis=2))
    return merge_groups(outs, lses).reshape(b, s, ATTN_WIDTH), new_cache


def setup_inputs(seed: int = 0) -> dict:
    key = jax.random.key(seed)
    ks = jax.random.split(key, 24)

    def nrm(k, shape, scale):
        return scale * jax.random.normal(k, shape, jnp.float32)

    buf = [min(win, PAST_LEN) for win, _ in GROUPS]
    return {
        'x_prompt': nrm(ks[0], (BATCH, SEQ, D_MODEL), 1.0),
        'x_sample': nrm(ks[1], (DEC_BATCH, DEC_SEQ, D_MODEL), 1.0),
        'state_sc_conv': nrm(ks[2], (N_CONV_LAYERS, DEC_BATCH, SC_WIDTH - 1, D_MODEL), 1.0),
        'cache_kv_d1': nrm(ks[3], (N_ATTN_LAYERS, DEC_BATCH, buf[0], 2, H_SLOT, HEAD_DIM), 1.0),
        'cache_kv_d4': nrm(ks[4], (N_ATTN_LAYERS, DEC_BATCH, buf[1], 2, H_SLOT, HEAD_DIM), 1.0),
        'cache_kv_d16': nrm(ks[5], (N_ATTN_LAYERS, DEC_BATCH, buf[2], 2, H_SLOT, HEAD_DIM), 1.0),
        'state_ffn_conv': nrm(ks[6], (DEPTH, DEC_BATCH, FFN_CONV_WIDTH - 1, D_FF), 1.0),
        'norm_mix': 1.0 + nrm(ks[7], (DEPTH, D_MODEL), 0.02),
        'norm_ffn': 1.0 + nrm(ks[8], (DEPTH, D_MODEL), 0.02),
        'sc_w_in': nrm(ks[9], (N_CONV_LAYERS, D_MODEL, 3 * D_MODEL), D_MODEL ** -0.5),
        'sc_conv_w': nrm(ks[10], (N_CONV_LAYERS, SC_WIDTH, D_MODEL), SC_WIDTH ** -0.5),
        'sc_w_out': nrm(ks[11], (N_CONV_LAYERS, D_MODEL, D_MODEL), D_MODEL ** -0.5),
        'attn_w_qkv': nrm(ks[12], (N_ATTN_LAYERS, D_MODEL, QKV_WIDTH), D_MODEL ** -0.5),
        'attn_q_norm': 1.0 + nrm(ks[13], (N_ATTN_LAYERS, HEAD_DIM), 0.02),
        'attn_k_norm': 1.0 + nrm(ks[14], (N_ATTN_LAYERS, HEAD_DIM), 0.02),
        'attn_w_out': nrm(ks[15], (N_ATTN_LAYERS, ATTN_WIDTH, D_MODEL), ATTN_WIDTH ** -0.5),
        'rel_bias': nrm(ks[16], (N_BUCKETS, N_GROUPS * H_SLOT), 0.5),
        'ffn_w_gate': nrm(ks[17], (DEPTH, D_MODEL, D_FF), D_MODEL ** -0.5),
        'ffn_w_up': nrm(ks[18], (DEPTH, D_MODEL, D_FF), D_MODEL ** -0.5),
        'ffn_conv_w': nrm(ks[19], (DEPTH, FFN_CONV_WIDTH, D_FF), FFN_CONV_WIDTH ** -0.5),
        'ffn_conv_b': nrm(ks[20], (DEPTH, D_FF), 0.02),
        'ffn_w_down': nrm(ks[21], (DEPTH, D_FF, D_MODEL), D_FF ** -0.5),
    }


def reference(x_prompt, x_sample, state_sc_conv, cache_kv_d1, cache_kv_d4, cache_kv_d16, state_ffn_conv,
              norm_mix, norm_ffn, sc_w_in, sc_conv_w, sc_w_out, attn_w_qkv, attn_q_norm, attn_k_norm,
              attn_w_out, rel_bias, ffn_w_gate, ffn_w_up, ffn_conv_w, ffn_conv_b, ffn_w_down):
    biases = _group_biases(rel_bias)
    xp, xs = x_prompt, x_sample
    bp = xp.shape[0]
    sample_caches = (cache_kv_d1, cache_kv_d4, cache_kv_d16)
    p_sc, s_sc, p_ffn, s_ffn = [], [], [], []
    p_kv = [[] for _ in range(N_GROUPS)]
    s_kv = [[] for _ in range(N_GROUPS)]
    for i in range(DEPTH):
        j = i // N_MIXERS
        hp = rmsnorm(xp, norm_mix[i])
        hs = rmsnorm(xs, norm_mix[i])
        if i % N_MIXERS == 0:
            zero = jnp.zeros((bp, SC_WIDTH - 1, D_MODEL), xp.dtype)
            mp, hist_p = shortconv_mixer(hp, zero, sc_w_in[j], sc_conv_w[j], sc_w_out[j])
            ms, hist_s = shortconv_mixer(hs, state_sc_conv[j], sc_w_in[j], sc_conv_w[j], sc_w_out[j])
            p_sc.append(hist_p)
            s_sc.append(hist_s)
        else:
            qp, kp, vp = project_qkv(hp, attn_w_qkv[j], attn_q_norm[j], attn_k_norm[j])
            op, cp = dilated_attention_prompt(qp, kp, vp, biases)
            qs, ks_, vs = project_qkv(hs, attn_w_qkv[j], attn_q_norm[j], attn_k_norm[j])
            os_, cs = dilated_attention_sample(qs, ks_, vs, [c[j] for c in sample_caches], biases)
            mp = op @ attn_w_out[j]
            ms = os_ @ attn_w_out[j]
            for g in range(N_GROUPS):
                p_kv[g].append(cp[g])
                s_kv[g].append(cs[g])
        xp = xp + mp
        xs = xs + ms
        hp = rmsnorm(xp, norm_ffn[i])
        hs = rmsnorm(xs, norm_ffn[i])
        zero_f = jnp.zeros((bp, FFN_CONV_WIDTH - 1, D_FF), xp.dtype)
        fp, fh_p = convffn(hp, zero_f, ffn_w_gate[i], ffn_w_up[i], ffn_conv_w[i], ffn_conv_b[i], ffn_w_down[i])
        fs, fh_s = convffn(hs, state_ffn_conv[i], ffn_w_gate[i], ffn_w_up[i], ffn_conv_w[i], ffn_conv_b[i], ffn_w_down[i])
        p_ffn.append(fh_p)
        s_ffn.append(fh_s)
        xp = xp + fp
        xs = xs + fs
    return (xp, xs,
            jnp.stack(p_sc, axis=0), jnp.stack(p_kv[0], axis=0), jnp.stack(p_kv[1], axis=0),
            jnp.stack(p_kv[2], axis=0), jnp.stack(p_ffn, axis=0),
            jnp.stack(s_sc, axis=0), jnp.stack(s_kv[0], axis=0), jnp.stack(s_kv[1], axis=0),
            jnp.stack(s_kv[2], axis=0), jnp.stack(s_ffn, axis=0))
```

```python
import functools
import math

import numpy as np
import jax
import jax.numpy as jnp
from jax import lax
from jax.experimental import pallas as pl
from jax.experimental.pallas import tpu as pltpu

D_MODEL = 1024
D_FF = 2816
N_HEADS = 8
HEAD_DIM = 64
ATTN_WIDTH = N_HEADS * HEAD_DIM
GROUPS = ((128, 1), (512, 4), (2048, 16))
N_GROUPS = len(GROUPS)
N_KEYS = 128
N_BUCKETS = 32
MAX_DISTANCE = 2048
EPS = 1e-6
NEG = -1e30
SCALE = HEAD_DIM ** -0.5
SUPER = 2048
QBLK = 128
LANES = 128
BF = jnp.bfloat16
F32 = jnp.float32

V7X_VMEM_BYTES = 64 * 1024 * 1024
VMEM_LIMIT = V7X_VMEM_BYTES - 8 * 1024 * 1024


def _cparams(sem):
    return pltpu.CompilerParams(dimension_semantics=sem, vmem_limit_bytes=VMEM_LIMIT)


def _const_spec(shape):
    nd = len(shape)
    return pl.BlockSpec(shape, lambda *_: (0,) * nd, pipeline_mode=pl.Buffered(1))


def _dot(a, b):
    return jnp.dot(a, b, preferred_element_type=F32)


def _rms(x, g):
    r = lax.rsqrt(jnp.mean(x * x, axis=-1, keepdims=True) + EPS)
    return (x * r) * g


def _head_rms(x, gain, bd):
    x2 = x * x
    hi = x2.astype(BF)
    lo = (x2 - hi.astype(F32)).astype(BF)
    ss = _dot(hi, bd) + _dot(lo, bd)
    r = lax.rsqrt(ss * (1.0 / HEAD_DIM) + EPS)
    return (x * r) * gain


def _conv3(buf, cw_ref, tm):
    out = cw_ref[0:1, :] * buf[6:6 + tm, :]
    out = out + cw_ref[1:2, :] * buf[7:7 + tm, :]
    return out + cw_ref[2:3, :] * buf[8:8 + tm, :]


def _mixer_body(x_ref, g_ref, win_ref, cw_ref, wout_ref, o_ref, hist_ref, ubuf, *, tm, nt):
    t = pl.program_id(1)

    @pl.when(t == 0)
    def _():
        ubuf[0:8, :] = jnp.zeros((8, D_MODEL), F32)

    x = x_ref[0]
    h = _rms(x, g_ref[...]).astype(BF)
    bg = _dot(h, win_ref[:, 0:D_MODEL])
    cg = _dot(h, win_ref[:, D_MODEL:2 * D_MODEL])
    xv = _dot(h, win_ref[:, 2 * D_MODEL:3 * D_MODEL])
    ubuf[8:8 + tm, :] = cg * xv
    conv = _conv3(ubuf, cw_ref, tm)
    y = _dot((bg * conv).astype(BF), wout_ref[...])
    o_ref[0] = x + y
    tail = ubuf[tm:tm + 8, :]
    ubuf[0:8, :] = tail

    @pl.when(t == nt - 1)
    def _():
        hist_ref[0] = tail[6:8, :]


def _mixer_prompt(x, g, w_in, conv_w, w_out, tm=512):
    b, t, d = x.shape
    nt = t // tm
    return pl.pallas_call(
        functools.partial(_mixer_body, tm=tm, nt=nt),
        grid=(b, nt),
        in_specs=[
            pl.BlockSpec((1, tm, d), lambda i, j: (i, j, 0)),
            _const_spec((1, d)),
            _const_spec((d, 3 * d)),
            _const_spec((3, d)),
            _const_spec((d, d)),
        ],
        out_specs=[
            pl.BlockSpec((1, tm, d), lambda i, j: (i, j, 0)),
            pl.BlockSpec((1, 2, d), lambda i, j: (i, 0, 0)),
        ],
        out_shape=[
            jax.ShapeDtypeStruct((b, t, d), F32),
            jax.ShapeDtypeStruct((b, 2, d), F32),
        ],
        scratch_shapes=[pltpu.VMEM((tm + 8, d), F32)],
        compiler_params=_cparams(("arbitrary", "arbitrary")),
        name="mixer_prompt",
    )(x, g, w_in, conv_w, w_out)


def _ffn_body(*refs, tm, nt, has_pre):
    if has_pre:
        x_ref, a_ref, wa_ref = refs[:3]
        refs = refs[3:]
    else:
        x_ref = refs[0]
        refs = refs[1:]
    g_ref, wg_ref, wu_ref, cw_ref, cb_ref, wd_ref, o_ref, hist_ref, gbuf = refs
    t = pl.program_id(1)

    @pl.when(t == 0)
    def _():
        gbuf[0:8, :] = jnp.zeros((8, D_FF), F32)

    x = x_ref[0]
    if has_pre:
        x = x + _dot(a_ref[0].astype(BF), wa_ref[...])
    h = _rms(x, g_ref[...]).astype(BF)
    gbuf[8:8 + tm, :] = _dot(h, wg_ref[...])
    up = _dot(h, wu_ref[...])
    z = _conv3(gbuf, cw_ref, tm) + cb_ref[...]
    act = (z * jax.nn.sigmoid(z)) * up
    o_ref[0] = x + _dot(act.astype(BF), wd_ref[...])
    tail = gbuf[tm:tm + 8, :]
    gbuf[0:8, :] = tail

    @pl.when(t == nt - 1)
    def _():
        hist_ref[0] = tail[6:8, :]


def _ffn_prompt(x, pre, g, w_gate, w_up, conv_w, conv_b, w_down, tm=256):
    b, t, d = x.shape
    nt = t // tm
    row = lambda i, j: (i, j, 0)
    in_specs = [pl.BlockSpec((1, tm, d), row)]
    args = [x]
    if pre is not None:
        a, wa = pre
        in_specs += [pl.BlockSpec((1, tm, a.shape[-1]), row), _const_spec(wa.shape)]
        args += [a, wa]
    in_specs += [
        _const_spec((1, d)),
        _const_spec((d, D_FF)),
        _const_spec((d, D_FF)),
        _const_spec((3, D_FF)),
        _const_spec((1, D_FF)),
        _const_spec((D_FF, d)),
    ]
    args += [g, w_gate, w_up, conv_w, conv_b, w_down]
    return pl.pallas_call(
        functools.partial(_ffn_body, tm=tm, nt=nt, has_pre=pre is not None),
        grid=(b, nt),
        in_specs=in_specs,
        out_specs=[
            pl.BlockSpec((1, tm, d), row),
            pl.BlockSpec((1, 2, D_FF), lambda i, j: (i, 0, 0)),
        ],
        out_shape=[
            jax.ShapeDtypeStruct((b, t, d), F32),
            jax.ShapeDtypeStruct((b, 2, D_FF), F32),
        ],
        scratch_shapes=[pltpu.VMEM((tm + 8, D_FF), F32)],
        compiler_params=_cparams(("arbitrary", "arbitrary")),
        name="ffn_prompt",
    )(*args)


def _qkv_body(x_ref, g_ref, w_ref, qg_ref, kg_ref, bd_ref, *refs, tm, nt):
    qkv_refs = refs[:3 * N_GROUPS]
    cache_refs = refs[3 * N_GROUPS:4 * N_GROUPS]
    scr = refs[4 * N_GROUPS]
    t = pl.program_id(1)
    h = _rms(x_ref[0], g_ref[...]).astype(BF)
    bd = bd_ref[...]
    for g, (win, dil) in enumerate(GROUPS):
        c0 = g * 3 * ATTN_WIDTH
        q = _dot(h, w_ref[:, c0:c0 + ATTN_WIDTH])
        k = _dot(h, w_ref[:, c0 + ATTN_WIDTH:c0 + 2 * ATTN_WIDTH])
        v = _dot(h, w_ref[:, c0 + 2 * ATTN_WIDTH:c0 + 3 * ATTN_WIDTH])
        q = _head_rms(q, qg_ref[...], bd)
        k = _head_rms(k, kg_ref[...], bd)

        c_ref = cache_refs[g]
        keep = win
        if keep >= tm:
            first = nt - keep // tm

            @pl.when(t >= first)
            def _():
                c_ref[0, :, 0:ATTN_WIDTH] = k
                c_ref[0, :, ATTN_WIDTH:2 * ATTN_WIDTH] = v
        else:
            @pl.when(t == nt - 1)
            def _():
                c_ref[0, :, 0:ATTN_WIDTH] = k[tm - keep:, :]
                c_ref[0, :, ATTN_WIDTH:2 * ATTN_WIDTH] = v[tm - keep:, :]

        outs = qkv_refs[3 * g:3 * g + 3]
        if dil == 1:
            for o_ref, val in zip(outs, (q, k, v)):
                o_ref[0, 0, 0] = val.astype(BF)
        else:
            rows = tm // dil
            chunks = ATTN_WIDTH // LANES
            for j, (o_ref, val) in enumerate(zip(outs, (q, k, v))):
                for cc in range(chunks):
                    scr[j * chunks + cc] = val[:, cc * LANES:(cc + 1) * LANES]
                for r in range(dil):
                    for cc in range(chunks):
                        o_ref[0, 0, r, :, cc * LANES:(cc + 1) * LANES] = (
                            scr[j * chunks + cc, pl.ds(r, rows, stride=dil), :].astype(BF))


def _qkv_prompt(x, g, w_qkv, q_gain, k_gain, bd, tm=256):
    b, t, d = x.shape
    nt = t // tm
    tiles_per_super = SUPER // tm
    out_specs, out_shapes = [], []
    for win, dil in GROUPS:
        rows = tm // dil
        shp = (b, t // SUPER, dil, SUPER // dil, ATTN_WIDTH)
        spec = pl.BlockSpec(
            (1, 1, dil, rows, ATTN_WIDTH),
            lambda i, j: (i, j // tiles_per_super, 0, j % tiles_per_super, 0))
        out_specs += [spec] * 3
        out_shapes += [jax.ShapeDtypeStruct(shp, BF)] * 3
    for win, dil in GROUPS:
        keep = win
        if keep >= tm:
            first = nt - keep // tm
            spec = pl.BlockSpec((1, tm, 2 * ATTN_WIDTH),
                                lambda i, j, first=first: (i, jnp.maximum(j - first, 0), 0))
        else:
            spec = pl.BlockSpec((1, keep, 2 * ATTN_WIDTH), lambda i, j: (i, 0, 0))
        out_specs.append(spec)
        out_shapes.append(jax.ShapeDtypeStruct((b, keep, 2 * ATTN_WIDTH), F32))
    res = pl.pallas_call(
        functools.partial(_qkv_body, tm=tm, nt=nt),
        grid=(b, nt),
        in_specs=[
            pl.BlockSpec((1, tm, d), lambda i, j: (i, j, 0)),
            _const_spec((1, d)),
            _const_spec(w_qkv.shape),
            _const_spec((1, ATTN_WIDTH)),
            _const_spec((1, ATTN_WIDTH)),
            _const_spec((ATTN_WIDTH, ATTN_WIDTH)),
        ],
        out_specs=out_specs,
        out_shape=out_shapes,
        scratch_shapes=[pltpu.VMEM((3 * ATTN_WIDTH // LANES, tm, LANES), F32)],
        compiler_params=_cparams(("arbitrary", "arbitrary")),
        name="qkv_prompt",
    )(x, g, w_qkv, q_gain, k_gain, bd)
    qkv = [r.reshape(b, t, ATTN_WIDTH) for r in res[:3 * N_GROUPS]]
    return qkv, res[3 * N_GROUPS:]


def _attn_body(*refs):
    qkv_refs = refs[:3 * N_GROUPS]
    tb_ref = refs[3 * N_GROUPS]
    o_ref = refs[3 * N_GROUPS + 1]
    acc_s, m_s, s_s = refs[3 * N_GROUPS + 2:]
    n_super = o_ref.shape[1] // SUPER
    blocks = SUPER // QBLK
    lane = lax.broadcasted_iota(jnp.int32, (QBLK, 2 * HEAD_DIM), 1)
    head_masks = (lane < HEAD_DIM, lane >= HEAD_DIM)
    contract_last = (((1,), (1,)), ((), ()))
    zero = jnp.zeros((), BF)

    for s in range(n_super):
        for g, (win, dil) in enumerate(GROUPS):
            q_ref, k_ref, v_ref = qkv_refs[3 * g:3 * g + 3]
            per_res = blocks // dil

            def block(n, carry, s=s, g=g, dil=dil, per_res=per_res,
                      q_ref=q_ref, k_ref=k_ref, v_ref=v_ref):
                c = n % per_res
                r = n // per_res
                start = s * SUPER + n * QBLK
                if s == 0:
                    prev = jnp.where(c > 0, start - QBLK, start)
                    first = jnp.where(c == 0, 1, 0)
                else:
                    prev = jnp.where(c > 0, start - QBLK, start - SUPER + SUPER // dil - QBLK)
                    first = 0
                start = pl.multiple_of(start, QBLK)
                prev = pl.multiple_of(prev, QBLK)
                qb = q_ref[0, pl.ds(start, QBLK), :]
                kc = k_ref[0, pl.ds(start, QBLK), :]
                kp = k_ref[0, pl.ds(prev, QBLK), :]
                vc = v_ref[0, pl.ds(start, QBLK), :]
                vp = v_ref[0, pl.ds(prev, QBLK), :]
                acc = jnp.zeros((QBLK, 2 * HEAD_DIM), F32)
                m_full = jnp.zeros((QBLK, 2 * HEAD_DIM), F32)
                s_full = jnp.zeros((QBLK, 2 * HEAD_DIM), F32)
                for hh in range(2):
                    msk = head_masks[hh]
                    qm = jnp.where(msk, qb, zero)
                    lp = lax.dot_general(qm, kp, contract_last, preferred_element_type=F32)
                    lc = lax.dot_general(qm, kc, contract_last, preferred_element_type=F32)
                    lp = lp + tb_ref[g, first, hh, :, 0:QBLK]
                    lc = lc + tb_ref[g, first, hh, :, QBLK:2 * QBLK]
                    m = jnp.maximum(jnp.max(lp, axis=1, keepdims=True),
                                    jnp.max(lc, axis=1, keepdims=True))
                    ep = jnp.exp(lp - m)
                    ec = jnp.exp(lc - m)
                    ssum = jnp.sum(ep, axis=1, keepdims=True) + jnp.sum(ec, axis=1, keepdims=True)
                    acc = acc + _dot(ep.astype(BF), jnp.where(msk, vp, zero))
                    acc = acc + _dot(ec.astype(BF), jnp.where(msk, vc, zero))
                    m_full = jnp.where(msk, m, m_full)
                    s_full = jnp.where(msk, ssum, s_full)
                nat = r + dil * QBLK * c
                if dil == 1:
                    rows = pl.ds(pl.multiple_of(nat, QBLK), QBLK)
                else:
                    rows = pl.ds(nat, QBLK, stride=dil)
                acc_s[g, rows, :] = acc
                m_s[g, rows, :] = m_full
                s_s[g, rows, :] = s_full
                return carry

            lax.fori_loop(0, blocks, block, 0)

        mrows = 256

        def merge(i, carry, s=s):
            rows = pl.ds(pl.multiple_of(i * mrows, mrows), mrows)
            m0, m1, m2 = m_s[0, rows, :], m_s[1, rows, :], m_s[2, rows, :]
            mm = jnp.maximum(jnp.maximum(m0, m1), m2)
            w0, w1, w2 = jnp.exp(m0 - mm), jnp.exp(m1 - mm), jnp.exp(m2 - mm)
            den = w0 * s_s[0, rows, :] + w1 * s_s[1, rows, :] + w2 * s_s[2, rows, :]
            num = w0 * acc_s[0, rows, :] + w1 * acc_s[1, rows, :] + w2 * acc_s[2, rows, :]
            o_ref[0, pl.ds(pl.multiple_of(s * SUPER + i * mrows, mrows), mrows), :] = num / den
            return carry

        lax.fori_loop(0, SUPER // mrows, merge, 0)


def _attn_prompt(qkv, tables):
    b, t, _ = qkv[0].shape
    pairs = N_HEADS // 2
    lanes = 2 * HEAD_DIM
    col = lambda i, j: (i, 0, j)
    return pl.pallas_call(
        _attn_body,
        grid=(b, pairs),
        in_specs=[pl.BlockSpec((1, t, lanes), col)] * (3 * N_GROUPS) + [
            pl.BlockSpec((N_GROUPS, 2, 2, QBLK, 2 * QBLK), lambda i, j: (0, 0, j, 0, 0)),
        ],
        out_specs=pl.BlockSpec((1, t, lanes), col),
        out_shape=jax.ShapeDtypeStruct((b, t, ATTN_WIDTH), F32),
        scratch_shapes=[pltpu.VMEM((N_GROUPS, SUPER, lanes), F32)] * 3,
        compiler_params=_cparams(("arbitrary", "arbitrary")),
        name="attn_prompt",
    )(*qkv, tables)


def _mixer_sample_body(x_ref, g_ref, win_ref, cw_ref, wout_ref, h0_ref, h1_ref, o_ref, u_ref):
    x = x_ref[...]
    h = _rms(x, g_ref[...]).astype(BF)
    bg = _dot(h, win_ref[:, 0:D_MODEL])
    cg = _dot(h, win_ref[:, D_MODEL:2 * D_MODEL])
    xv = _dot(h, win_ref[:, 2 * D_MODEL:3 * D_MODEL])
    u = cg * xv
    conv = cw_ref[0:1, :] * h0_ref[...]
    conv = conv + cw_ref[1:2, :] * h1_ref[...]
    conv = conv + cw_ref[2:3, :] * u
    o_ref[...] = x + _dot((bg * conv).astype(BF), wout_ref[...])
    u_ref[...] = u


def _mixer_sample(x, g, w_in, conv_w, w_out, h0, h1):
    n, d = x.shape
    return pl.pallas_call(
        _mixer_sample_body,
        out_shape=[jax.ShapeDtypeStruct((n, d), F32)] * 2,
        compiler_params=pltpu.CompilerParams(vmem_limit_bytes=VMEM_LIMIT),
        name="mixer_sample",
    )(x, g, w_in, conv_w, w_out, h0, h1)


def _ffn_sample_body(*refs, has_pre):
    if has_pre:
        x_ref, a_ref, wa_ref = refs[:3]
        refs = refs[3:]
    else:
        x_ref = refs[0]
        refs = refs[1:]
    g_ref, wg_ref, wu_ref, cw_ref, cb_ref, wd_ref, h0_ref, h1_ref, o_ref, gate_ref = refs
    x = x_ref[...]
    if has_pre:
        x = x + _dot(a_ref[...].astype(BF), wa_ref[...])
    h = _rms(x, g_ref[...]).astype(BF)
    gate = _dot(h, wg_ref[...])
    up = _dot(h, wu_ref[...])
    z = cw_ref[0:1, :] * h0_ref[...]
    z = z + cw_ref[1:2, :] * h1_ref[...]
    z = z + cw_ref[2:3, :] * gate
    z = z + cb_ref[...]
    act = (z * jax.nn.sigmoid(z)) * up
    o_ref[...] = x + _dot(act.astype(BF), wd_ref[...])
    gate_ref[...] = gate


def _ffn_sample(x, pre, g, w_gate, w_up, conv_w, conv_b, w_down, h0, h1):
    n, d = x.shape
    args = [x] + (list(pre) if pre is not None else [])
    args += [g, w_gate, w_up, conv_w, conv_b, w_down, h0, h1]
    return pl.pallas_call(
        functools.partial(_ffn_sample_body, has_pre=pre is not None),
        out_shape=[jax.ShapeDtypeStruct((n, d), F32), jax.ShapeDtypeStruct((n, D_FF), F32)],
        compiler_params=pltpu.CompilerParams(vmem_limit_bytes=VMEM_LIMIT),
        name="ffn_sample",
    )(*args)


def _qkv_sample_body(x_ref, g_ref, w_ref, qg_ref, kg_ref, bd_ref, q_ref, kv_ref):
    h = _rms(x_ref[...], g_ref[...]).astype(BF)
    bd = bd_ref[...]
    for g in range(N_GROUPS):
        c0 = g * 3 * ATTN_WIDTH
        q = _dot(h, w_ref[:, c0:c0 + ATTN_WIDTH])
        k = _dot(h, w_ref[:, c0 + ATTN_WIDTH:c0 + 2 * ATTN_WIDTH])
        v = _dot(h, w_ref[:, c0 + 2 * ATTN_WIDTH:c0 + 3 * ATTN_WIDTH])
        q_ref[:, g * ATTN_WIDTH:(g + 1) * ATTN_WIDTH] = _head_rms(q, qg_ref[...], bd)
        kv_ref[:, 2 * g * ATTN_WIDTH:(2 * g + 1) * ATTN_WIDTH] = _head_rms(k, kg_ref[...], bd)
        kv_ref[:, (2 * g + 1) * ATTN_WIDTH:(2 * g + 2) * ATTN_WIDTH] = v


def _qkv_sample(x, g, w_qkv, q_gain, k_gain, bd):
    n = x.shape[0]
    return pl.pallas_call(
        _qkv_sample_body,
        out_shape=[jax.ShapeDtypeStruct((n, N_GROUPS * ATTN_WIDTH), F32),
                   jax.ShapeDtypeStruct((n, N_GROUPS * 2 * ATTN_WIDTH), F32)],
        compiler_params=pltpu.CompilerParams(vmem_limit_bytes=VMEM_LIMIT),
        name="qkv_sample",
    )(x, g, w_qkv, q_gain, k_gain, bd)


def _attn_sample_body(q_ref, kvn_ref, c0_ref, c1_ref, c2_ref, sb_ref, sb0_ref, o_ref):
    ms, ss, nums = [], [], []
    for g, c_ref in enumerate((c0_ref, c1_ref, c2_ref)):
        q = q_ref[0, g]
        kc = c_ref[0, 0, :, 0]
        vc = c_ref[0, 0, :, 1]
        kn = kvn_ref[0, g, 0]
        vn = kvn_ref[0, g, 1]
        lc = jnp.sum(kc * q[None], axis=-1, keepdims=True) + sb_ref[g]
        ln = jnp.sum(kn * q, axis=-1, keepdims=True) + sb0_ref[g]
        m = jnp.maximum(jnp.max(lc, axis=0), ln)
        ec = jnp.exp(lc - m[None])
        en = jnp.exp(ln - m)
        ss.append(jnp.sum(ec, axis=0) + en)
        nums.append(jnp.sum(ec * vc, axis=0) + en * vn)
        ms.append(m)
    mm = jnp.maximum(jnp.maximum(ms[0], ms[1]), ms[2])
    w = [jnp.exp(m - mm) for m in ms]
    den = w[0] * ss[0] + w[1] * ss[1] + w[2] * ss[2]
    num = w[0] * nums[0] + w[1] * nums[1] + w[2] * nums[2]
    o_ref[0] = num / den


def _attn_sample(q, kvn, caches, layer, sb, sb0):
    n = q.shape[0]
    cache_specs = []
    views = []
    for c, (win, dil) in zip(caches, GROUPS):
        nl, nb, keep = c.shape[:3]
        assert keep == win, "the cache must hold one full window"
        views.append(c.reshape(nl, nb, N_KEYS, dil, 2, N_HEADS, HEAD_DIM))
        cache_specs.append(pl.BlockSpec(
            (1, 1, N_KEYS, None, 2, N_HEADS, HEAD_DIM),
            lambda i, layer=layer: (layer, i, 0, 0, 0, 0, 0)))
    return pl.pallas_call(
        _attn_sample_body,
        grid=(n,),
        in_specs=[
            pl.BlockSpec((1, N_GROUPS, N_HEADS, HEAD_DIM), lambda i: (i, 0, 0, 0)),
            pl.BlockSpec((1, N_GROUPS, 2, N_HEADS, HEAD_DIM), lambda i: (i, 0, 0, 0, 0)),
            *cache_specs,
            pl.BlockSpec(sb.shape, lambda i: (0, 0, 0, 0)),
            pl.BlockSpec(sb0.shape, lambda i: (0, 0, 0)),
        ],
        out_specs=pl.BlockSpec((1, N_HEADS, HEAD_DIM), lambda i: (i, 0, 0)),
        out_shape=jax.ShapeDtypeStruct((n, N_HEADS, HEAD_DIM), F32),
        compiler_params=_cparams(("arbitrary",)),
        name="attn_sample",
    )(q, kvn, *views, sb, sb0)


def _shift_body(*refs):
    caches, news, outs, sem = refs[0:3], refs[3:6], refs[6:9], refs[9]
    copies = []
    for g, (c_ref, n_ref, o_ref) in enumerate(zip(caches, news, outs)):
        keep = c_ref.shape[2]
        for layer in range(c_ref.shape[0]):
            copies.append(pltpu.make_async_copy(
                c_ref.at[layer, :, pl.ds(1, keep - 1)], o_ref.at[layer, :, pl.ds(0, keep - 1)],
                sem.at[len(copies)]))
            copies.append(pltpu.make_async_copy(
                n_ref.at[layer], o_ref.at[layer, :, pl.ds(keep - 1, 1)], sem.at[len(copies)]))
    for cp in copies:
        cp.start()
    for cp in copies:
        cp.wait()


def _shift_caches(caches, news):
    n_copies = sum(2 * c.shape[0] for c in caches)
    any_spec = pl.BlockSpec(memory_space=pl.ANY)
    return pl.pallas_call(
        _shift_body,
        in_specs=[any_spec] * 6,
        out_specs=[any_spec] * 3,
        out_shape=[jax.ShapeDtypeStruct(c.shape, c.dtype) for c in caches],
        scratch_shapes=[pltpu.SemaphoreType.DMA((n_copies,))],
        name="shift_caches",
    )(*caches, *news)


def _t5_bucket(dist):
    exact = N_BUCKETS // 2
    n = np.asarray(dist, dtype=np.float32)
    large = exact + np.log(np.maximum(n, 1.0) / exact) / math.log(MAX_DISTANCE / exact) * (N_BUCKETS - exact)
    large = np.minimum(np.floor(large), N_BUCKETS - 1)
    return np.where(n < exact, n, large).astype(np.int32)


def _bias_tables(rel_bias):
    qi = np.arange(QBLK)[:, None]
    ci = np.arange(2 * QBLK)[None, :]
    j = qi + QBLK - ci
    valid = (j >= 0) & (j <= N_KEYS)
    jc = np.clip(j, 0, N_KEYS)
    tabs, sbs, sb0s = [], [], []
    for g, (win, dil) in enumerate(GROUPS):
        bucket = _t5_bucket(dil * np.arange(N_KEYS + 1))
        bias = jnp.take(rel_bias, bucket, axis=0)[:, g * N_HEADS:(g + 1) * N_HEADS].T
        band = jnp.where(valid[None], bias[:, jc], NEG)
        band_first = jnp.where((ci >= QBLK)[None], band, NEG)
        tabs.append(jnp.stack([band, band_first], axis=0))
        sbs.append(bias[:, N_KEYS:0:-1].T[:, :, None])
        sb0s.append(bias[:, 0:1])
    return jnp.stack(tabs), jnp.stack(sbs), jnp.stack(sb0s)


def kernel(x_prompt, x_sample, state_sc_conv, cache_kv_d1, cache_kv_d4, cache_kv_d16, state_ffn_conv,
           norm_mix, norm_ffn, sc_w_in, sc_conv_w, sc_w_out, attn_w_qkv, attn_q_norm, attn_k_norm,
           attn_w_out, rel_bias, ffn_w_gate, ffn_w_up, ffn_conv_w, ffn_conv_b, ffn_w_down):
    depth = norm_mix.shape[0]
    nb = x_sample.shape[0]
    caches = (cache_kv_d1, cache_kv_d4, cache_kv_d16)
    tables, sb, sb0 = _bias_tables(rel_bias)
    seg = np.arange(ATTN_WIDTH) // HEAD_DIM
    bd = jnp.asarray(seg[:, None] == seg[None, :], BF)

    sc_w_in_b, sc_w_out_b = sc_w_in.astype(BF), sc_w_out.astype(BF)
    w_qkv_b, w_ao_b = attn_w_qkv.astype(BF), attn_w_out.astype(BF)
    w_gate_b, w_up_b, w_down_b = ffn_w_gate.astype(BF), ffn_w_up.astype(BF), ffn_w_down.astype(BF)

    xp = x_prompt
    xs = x_sample[:, 0, :]
    p_sc, s_sc, p_ffn, s_ffn = [], [], [], []
    p_kv = [[] for _ in range(N_GROUPS)]
    s_new = [[] for _ in range(N_GROUPS)]
    for i in range(depth):
        j = i // 2
        g_mix = norm_mix[i][None]
        g_ffn = norm_ffn[i][None]
        ffn_w = (g_ffn, w_gate_b[i], w_up_b[i], ffn_conv_w[i], ffn_conv_b[i][None], w_down_b[i])
        sf = state_ffn_conv[i]
        if i % 2 == 0:
            xp, hist = _mixer_prompt(xp, g_mix, sc_w_in_b[j], sc_conv_w[j], sc_w_out_b[j])
            p_sc.append(hist)
            st = state_sc_conv[j]
            xs, u = _mixer_sample(xs, g_mix, sc_w_in_b[j], sc_conv_w[j], sc_w_out_b[j], st[:, 0], st[:, 1])
            s_sc.append(jnp.stack([st[:, 1], u], axis=1))
            pre_p = pre_s = None
        else:
            q_gain = (jnp.tile(attn_q_norm[j], N_HEADS) * SCALE)[None]
            k_gain = jnp.tile(attn_k_norm[j], N_HEADS)[None]
            qkv, kv_rows = _qkv_prompt(xp, g_mix, w_qkv_b[j], q_gain, k_gain, bd)
            for g in range(N_GROUPS):
                p_kv[g].append(kv_rows[g].reshape(kv_rows[g].shape[:2] + (2, N_HEADS, HEAD_DIM)))
            pre_p = (_attn_prompt(qkv, tables), w_ao_b[j])

            qs, kvs = _qkv_sample(xs, g_mix, w_qkv_b[j], q_gain, k_gain, bd)
            qs = qs.reshape(nb, N_GROUPS, N_HEADS, HEAD_DIM)
            kvs = kvs.reshape(nb, N_GROUPS, 2, N_HEADS, HEAD_DIM)
            for g in range(N_GROUPS):
                s_new[g].append(kvs[:, g][:, None])
            a_s = _attn_sample(qs, kvs, caches, j, sb, sb0)
            pre_s = (a_s.reshape(nb, ATTN_WIDTH), w_ao_b[j])
        xp, fh = _ffn_prompt(xp, pre_p, *ffn_w)
        p_ffn.append(fh)
        xs, gate = _ffn_sample(xs, pre_s, *ffn_w, sf[:, 0], sf[:, 1])
        s_ffn.append(jnp.stack([sf[:, 1], gate], axis=1))

    s_kv = _shift_caches(caches, [jnp.stack(n, axis=0) for n in s_new])
    return (xp, xs[:, None, :],
            jnp.stack(p_sc, axis=0), jnp.stack(p_kv[0], axis=0), jnp.stack(p_kv[1], axis=0),
            jnp.stack(p_kv[2], axis=0), jnp.stack(p_ffn, axis=0),
            jnp.stack(s_sc, axis=0), s_kv[0], s_kv[1], s_kv[2], jnp.stack(s_ffn, axis=0))
```

```python
import functools
import math

import numpy as np
import jax
import jax.numpy as jnp
from jax import lax
from jax.experimental import pallas as pl
from jax.experimental.pallas import tpu as pltpu

D_MODEL = 1024
D_FF = 2816
N_HEADS = 8
HEAD_DIM = 64
ATTN_WIDTH = N_HEADS * HEAD_DIM
GROUPS = ((128, 1), (512, 4), (2048, 16))
N_GROUPS = len(GROUPS)
N_KEYS = 128
N_BUCKETS = 32
MAX_DISTANCE = 2048
EPS = 1e-6
NEG = -1e30
SCALE = HEAD_DIM ** -0.5
SUPER = 2048
QBLK = 128
LANES = 128
ATTN_UNROLL = 4
BF = jnp.bfloat16
F32 = jnp.float32

V7X_VMEM_BYTES = 64 * 1024 * 1024
VMEM_LIMIT = V7X_VMEM_BYTES - 8 * 1024 * 1024


def _cparams(sem):
    return pltpu.CompilerParams(dimension_semantics=sem, vmem_limit_bytes=VMEM_LIMIT)


def _const_spec(shape):
    nd = len(shape)
    return pl.BlockSpec(shape, lambda *_: (0,) * nd, pipeline_mode=pl.Buffered(1))


def _dot(a, b):
    return jnp.dot(a, b, preferred_element_type=F32)


def _rms(x, g):
    r = lax.rsqrt(jnp.mean(x * x, axis=-1, keepdims=True) + EPS)
    return (x * r) * g


def _head_rms(x, gain, bd):
    x2 = x * x
    hi = x2.astype(BF)
    lo = (x2 - hi.astype(F32)).astype(BF)
    ss = _dot(hi, bd) + _dot(lo, bd)
    r = lax.rsqrt(ss * (1.0 / HEAD_DIM) + EPS)
    return (x * r) * gain


def _conv3(buf, cw_ref, tm):
    out = cw_ref[0:1, :] * buf[6:6 + tm, :]
    out = out + cw_ref[1:2, :] * buf[7:7 + tm, :]
    return out + cw_ref[2:3, :] * buf[8:8 + tm, :]


def _mixer_body(x_ref, g_ref, win_ref, cw_ref, wout_ref, o_ref, hist_ref, ubuf, *, tm, nt):
    t = pl.program_id(1)

    @pl.when(t == 0)
    def _():
        ubuf[0:8, :] = jnp.zeros((8, D_MODEL), F32)

    x = x_ref[0]
    h = _rms(x, g_ref[...]).astype(BF)
    bg = _dot(h, win_ref[:, 0:D_MODEL])
    cg = _dot(h, win_ref[:, D_MODEL:2 * D_MODEL])
    xv = _dot(h, win_ref[:, 2 * D_MODEL:3 * D_MODEL])
    ubuf[8:8 + tm, :] = cg * xv
    conv = _conv3(ubuf, cw_ref, tm)
    y = _dot((bg * conv).astype(BF), wout_ref[...])
    o_ref[0] = x + y
    tail = ubuf[tm:tm + 8, :]
    ubuf[0:8, :] = tail

    @pl.when(t == nt - 1)
    def _():
        hist_ref[0] = tail[6:8, :]


def _mixer_prompt(x, g, w_in, conv_w, w_out, tm=512):
    b, t, d = x.shape
    nt = t // tm
    return pl.pallas_call(
        functools.partial(_mixer_body, tm=tm, nt=nt),
        grid=(b, nt),
        in_specs=[
            pl.BlockSpec((1, tm, d), lambda i, j: (i, j, 0)),
            _const_spec((1, d)),
            _const_spec((d, 3 * d)),
            _const_spec((3, d)),
            _const_spec((d, d)),
        ],
        out_specs=[
            pl.BlockSpec((1, tm, d), lambda i, j: (i, j, 0)),
            pl.BlockSpec((1, 2, d), lambda i, j: (i, 0, 0)),
        ],
        out_shape=[
            jax.ShapeDtypeStruct((b, t, d), F32),
            jax.ShapeDtypeStruct((b, 2, d), F32),
        ],
        scratch_shapes=[pltpu.VMEM((tm + 8, d), F32)],
        compiler_params=_cparams(("arbitrary", "arbitrary")),
        name="mixer_prompt",
    )(x, g, w_in, conv_w, w_out)


def _ffn_body(*refs, tm, nt, has_pre):
    if has_pre:
        x_ref, a_ref, wa_ref = refs[:3]
        refs = refs[3:]
    else:
        x_ref = refs[0]
        refs = refs[1:]
    g_ref, wg_ref, wu_ref, cw_ref, cb_ref, wd_ref, o_ref, hist_ref, gbuf = refs
    t = pl.program_id(1)

    @pl.when(t == 0)
    def _():
        gbuf[0:8, :] = jnp.zeros((8, D_FF), F32)

    x = x_ref[0]
    if has_pre:
        x = x + _dot(a_ref[0].astype(BF), wa_ref[...])
    h = _rms(x, g_ref[...]).astype(BF)
    gbuf[8:8 + tm, :] = _dot(h, wg_ref[...])
    up = _dot(h, wu_ref[...])
    z = _conv3(gbuf, cw_ref, tm) + cb_ref[...]
    act = (z * jax.nn.sigmoid(z)) * up
    o_ref[0] = x + _dot(act.astype(BF), wd_ref[...])
    tail = gbuf[tm:tm + 8, :]
    gbuf[0:8, :] = tail

    @pl.when(t == nt - 1)
    def _():
        hist_ref[0] = tail[6:8, :]


def _ffn_prompt(x, pre, g, w_gate, w_up, conv_w, conv_b, w_down, tm=256):
    b, t, d = x.shape
    nt = t // tm
    row = lambda i, j: (i, j, 0)
    in_specs = [pl.BlockSpec((1, tm, d), row)]
    args = [x]
    if pre is not None:
        a, wa = pre
        in_specs += [pl.BlockSpec((1, tm, a.shape[-1]), row), _const_spec(wa.shape)]
        args += [a, wa]
    in_specs += [
        _const_spec((1, d)),
        _const_spec((d, D_FF)),
        _const_spec((d, D_FF)),
        _const_spec((3, D_FF)),
        _const_spec((1, D_FF)),
        _const_spec((D_FF, d)),
    ]
    args += [g, w_gate, w_up, conv_w, conv_b, w_down]
    return pl.pallas_call(
        functools.partial(_ffn_body, tm=tm, nt=nt, has_pre=pre is not None),
        grid=(b, nt),
        in_specs=in_specs,
        out_specs=[
            pl.BlockSpec((1, tm, d), row),
            pl.BlockSpec((1, 2, D_FF), lambda i, j: (i, 0, 0)),
        ],
        out_shape=[
            jax.ShapeDtypeStruct((b, t, d), F32),
            jax.ShapeDtypeStruct((b, 2, D_FF), F32),
        ],
        scratch_shapes=[pltpu.VMEM((tm + 8, D_FF), F32)],
        compiler_params=_cparams(("arbitrary", "arbitrary")),
        name="ffn_prompt",
    )(*args)


def _qkv_body(x_ref, g_ref, w_ref, qg_ref, kg_ref, bd_ref, *refs, tm, nt):
    qkv_refs = refs[:3 * N_GROUPS]
    cache_refs = refs[3 * N_GROUPS:4 * N_GROUPS]
    scr = refs[4 * N_GROUPS]
    t = pl.program_id(1)
    h = _rms(x_ref[0], g_ref[...]).astype(BF)
    bd = bd_ref[...]
    for g, (win, dil) in enumerate(GROUPS):
        c0 = g * 3 * ATTN_WIDTH
        q = _dot(h, w_ref[:, c0:c0 + ATTN_WIDTH])
        k = _dot(h, w_ref[:, c0 + ATTN_WIDTH:c0 + 2 * ATTN_WIDTH])
        v = _dot(h, w_ref[:, c0 + 2 * ATTN_WIDTH:c0 + 3 * ATTN_WIDTH])
        q = _head_rms(q, qg_ref[...], bd)
        k = _head_rms(k, kg_ref[...], bd)

        c_ref = cache_refs[g]
        keep = win
        if keep >= tm:
            first = nt - keep // tm

            @pl.when(t >= first)
            def _():
                c_ref[0, :, 0:ATTN_WIDTH] = k
                c_ref[0, :, ATTN_WIDTH:2 * ATTN_WIDTH] = v
        else:
            @pl.when(t == nt - 1)
            def _():
                c_ref[0, :, 0:ATTN_WIDTH] = k[tm - keep:, :]
                c_ref[0, :, ATTN_WIDTH:2 * ATTN_WIDTH] = v[tm - keep:, :]

        outs = qkv_refs[3 * g:3 * g + 3]
        if dil == 1:
            for o_ref, val in zip(outs, (q, k, v)):
                o_ref[0, 0, 0] = val.astype(BF)
        else:
            rows = tm // dil
            chunks = ATTN_WIDTH // LANES
            for j, (o_ref, val) in enumerate(zip(outs, (q, k, v))):
                for cc in range(chunks):
                    scr[j * chunks + cc] = val[:, cc * LANES:(cc + 1) * LANES]
                for r in range(dil):
                    for cc in range(chunks):
                        o_ref[0, 0, r, :, cc * LANES:(cc + 1) * LANES] = (
                            scr[j * chunks + cc, pl.ds(r, rows, stride=dil), :].astype(BF))


def _qkv_prompt(x, g, w_qkv, q_gain, k_gain, bd, tm=256):
    b, t, d = x.shape
    nt = t // tm
    tiles_per_super = SUPER // tm
    out_specs, out_shapes = [], []
    for win, dil in GROUPS:
        rows = tm // dil
        shp = (b, t // SUPER, dil, SUPER // dil, ATTN_WIDTH)
        spec = pl.BlockSpec(
            (1, 1, dil, rows, ATTN_WIDTH),
            lambda i, j: (i, j // tiles_per_super, 0, j % tiles_per_super, 0))
        out_specs += [spec] * 3
        out_shapes += [jax.ShapeDtypeStruct(shp, BF)] * 3
    for win, dil in GROUPS:
        keep = win
        if keep >= tm:
            first = nt - keep // tm
            spec = pl.BlockSpec((1, tm, 2 * ATTN_WIDTH),
                                lambda i, j, first=first: (i, jnp.maximum(j - first, 0), 0))
        else:
            spec = pl.BlockSpec((1, keep, 2 * ATTN_WIDTH), lambda i, j: (i, 0, 0))
        out_specs.append(spec)
        out_shapes.append(jax.ShapeDtypeStruct((b, keep, 2 * ATTN_WIDTH), F32))
    res = pl.pallas_call(
        functools.partial(_qkv_body, tm=tm, nt=nt),
        grid=(b, nt),
        in_specs=[
            pl.BlockSpec((1, tm, d), lambda i, j: (i, j, 0)),
            _const_spec((1, d)),
            _const_spec(w_qkv.shape),
            _const_spec((1, ATTN_WIDTH)),
            _const_spec((1, ATTN_WIDTH)),
            _const_spec((ATTN_WIDTH, ATTN_WIDTH)),
        ],
        out_specs=out_specs,
        out_shape=out_shapes,
        scratch_shapes=[pltpu.VMEM((3 * ATTN_WIDTH // LANES, tm, LANES), F32)],
        compiler_params=_cparams(("arbitrary", "arbitrary")),
        name="qkv_prompt",
    )(x, g, w_qkv, q_gain, k_gain, bd)
    qkv = [r.reshape(b, t, ATTN_WIDTH) for r in res[:3 * N_GROUPS]]
    return qkv, res[3 * N_GROUPS:]


def _attn_body(*refs):
    qkv_refs = refs[:3 * N_GROUPS]
    tb_ref = refs[3 * N_GROUPS]
    o_ref = refs[3 * N_GROUPS + 1]
    acc_s, m_s, s_s = refs[3 * N_GROUPS + 2:]
    n_super = o_ref.shape[1] // SUPER
    blocks = SUPER // QBLK
    lane = lax.broadcasted_iota(jnp.int32, (QBLK, LANES), 1)
    first_head = lane < HEAD_DIM
    contract_last = (((1,), (1,)), ((), ()))
    zero = jnp.zeros((), BF)
    ones = jnp.ones((2 * QBLK, LANES), BF)

    for s in range(n_super):
        for g, (win, dil) in enumerate(GROUPS):
            q_ref, k_ref, v_ref = qkv_refs[3 * g:3 * g + 3]
            per_res = blocks // dil

            def trip(it, carry, s=s, g=g, dil=dil, per_res=per_res,
                     q_ref=q_ref, k_ref=k_ref, v_ref=v_ref):
                staged = []
                for u in range(ATTN_UNROLL):
                    n = it * ATTN_UNROLL + u
                    c = n % per_res
                    r = n // per_res
                    start = s * SUPER + n * QBLK
                    if s == 0:
                        prev = jnp.where(c > 0, start - QBLK, start)
                        first = jnp.where(c == 0, 1, 0)
                    else:
                        prev = jnp.where(c > 0, start - QBLK, start - SUPER + SUPER // dil - QBLK)
                        first = 0
                    start = pl.multiple_of(start, QBLK)
                    prev = pl.multiple_of(prev, QBLK)
                    qb = q_ref[0, pl.ds(start, QBLK), :]
                    q2 = jnp.concatenate(
                        [jnp.where(first_head, qb, zero), jnp.where(first_head, zero, qb)], axis=0)
                    k2 = jnp.concatenate(
                        [k_ref[0, pl.ds(prev, QBLK), :], k_ref[0, pl.ds(start, QBLK), :]], axis=0)
                    logits = lax.dot_general(q2, k2, contract_last, preferred_element_type=F32)
                    staged.append((logits + tb_ref[g, first, 0], start, prev, r + dil * QBLK * c))
                probs = []
                for logits, start, prev, nat in staged:
                    m = jnp.max(logits, axis=1, keepdims=True)
                    probs.append((jnp.exp(logits - m).astype(BF), m, start, prev, nat))
                for e, m, start, prev, nat in probs:
                    v2 = jnp.concatenate(
                        [v_ref[0, pl.ds(prev, QBLK), :], v_ref[0, pl.ds(start, QBLK), :]], axis=0)
                    pv = _dot(e, jnp.concatenate([v2, ones], axis=1))
                    if dil == 1:
                        rows = pl.ds(pl.multiple_of(nat, QBLK), QBLK)
                    else:
                        rows = pl.ds(nat, QBLK, stride=dil)
                    acc_s[g, rows, :] = jnp.where(first_head, pv[0:QBLK, 0:LANES], pv[QBLK:, 0:LANES])
                    s_s[g, rows, :] = jnp.where(first_head, pv[0:QBLK, LANES:], pv[QBLK:, LANES:])
                    m_s[g, rows, :] = jnp.where(first_head, m[0:QBLK], m[QBLK:])
                return carry

            lax.fori_loop(0, blocks // ATTN_UNROLL, trip, 0)

        mrows = 256

        def merge(i, carry, s=s):
            rows = pl.ds(pl.multiple_of(i * mrows, mrows), mrows)
            m0, m1, m2 = m_s[0, rows, :], m_s[1, rows, :], m_s[2, rows, :]
            mm = jnp.maximum(jnp.maximum(m0, m1), m2)
            w0, w1, w2 = jnp.exp(m0 - mm), jnp.exp(m1 - mm), jnp.exp(m2 - mm)
            den = w0 * s_s[0, rows, :] + w1 * s_s[1, rows, :] + w2 * s_s[2, rows, :]
            num = w0 * acc_s[0, rows, :] + w1 * acc_s[1, rows, :] + w2 * acc_s[2, rows, :]
            o_ref[0, pl.ds(pl.multiple_of(s * SUPER + i * mrows, mrows), mrows), :] = num / den
            return carry

        lax.fori_loop(0, SUPER // mrows, merge, 0)


def _attn_prompt(qkv, tables):
    b, t, _ = qkv[0].shape
    pairs = N_HEADS // 2
    col = lambda i, j: (i, 0, j)
    return pl.pallas_call(
        _attn_body,
        grid=(b, pairs),
        in_specs=[pl.BlockSpec((1, t, LANES), col)] * (3 * N_GROUPS) + [
            pl.BlockSpec((N_GROUPS, 2, 1, 2 * QBLK, 2 * QBLK), lambda i, j: (0, 0, j, 0, 0)),
        ],
        out_specs=pl.BlockSpec((1, t, LANES), col),
        out_shape=jax.ShapeDtypeStruct((b, t, ATTN_WIDTH), F32),
        scratch_shapes=[pltpu.VMEM((N_GROUPS, SUPER, LANES), F32)] * 3,
        compiler_params=_cparams(("arbitrary", "arbitrary")),
        name="attn_prompt",
    )(*qkv, tables)


def _mixer_sample_body(x_ref, g_ref, win_ref, cw_ref, wout_ref, h0_ref, h1_ref, o_ref, u_ref):
    x = x_ref[...]
    h = _rms(x, g_ref[...]).astype(BF)
    bg = _dot(h, win_ref[:, 0:D_MODEL])
    cg = _dot(h, win_ref[:, D_MODEL:2 * D_MODEL])
    xv = _dot(h, win_ref[:, 2 * D_MODEL:3 * D_MODEL])
    u = cg * xv
    conv = cw_ref[0:1, :] * h0_ref[...]
    conv = conv + cw_ref[1:2, :] * h1_ref[...]
    conv = conv + cw_ref[2:3, :] * u
    o_ref[...] = x + _dot((bg * conv).astype(BF), wout_ref[...])
    u_ref[...] = u


def _mixer_sample(x, g, w_in, conv_w, w_out, h0, h1):
    n, d = x.shape
    return pl.pallas_call(
        _mixer_sample_body,
        out_shape=[jax.ShapeDtypeStruct((n, d), F32)] * 2,
        compiler_params=pltpu.CompilerParams(vmem_limit_bytes=VMEM_LIMIT),
        name="mixer_sample",
    )(x, g, w_in, conv_w, w_out, h0, h1)


def _ffn_sample_body(*refs, has_pre):
    if has_pre:
        x_ref, a_ref, wa_ref = refs[:3]
        refs = refs[3:]
    else:
        x_ref = refs[0]
        refs = refs[1:]
    g_ref, wg_ref, wu_ref, cw_ref, cb_ref, wd_ref, h0_ref, h1_ref, o_ref, gate_ref = refs
    x = x_ref[...]
    if has_pre:
        x = x + _dot(a_ref[...].astype(BF), wa_ref[...])
    h = _rms(x, g_ref[...]).astype(BF)
    gate = _dot(h, wg_ref[...])
    up = _dot(h, wu_ref[...])
    z = cw_ref[0:1, :] * h0_ref[...]
    z = z + cw_ref[1:2, :] * h1_ref[...]
    z = z + cw_ref[2:3, :] * gate
    z = z + cb_ref[...]
    act = (z * jax.nn.sigmoid(z)) * up
    o_ref[...] = x + _dot(act.astype(BF), wd_ref[...])
    gate_ref[...] = gate


def _ffn_sample(x, pre, g, w_gate, w_up, conv_w, conv_b, w_down, h0, h1):
    n, d = x.shape
    args = [x] + (list(pre) if pre is not None else [])
    args += [g, w_gate, w_up, conv_w, conv_b, w_down, h0, h1]
    return pl.pallas_call(
        functools.partial(_ffn_sample_body, has_pre=pre is not None),
        out_shape=[jax.ShapeDtypeStruct((n, d), F32), jax.ShapeDtypeStruct((n, D_FF), F32)],
        compiler_params=pltpu.CompilerParams(vmem_limit_bytes=VMEM_LIMIT),
        name="ffn_sample",
    )(*args)


def _qkv_sample_body(x_ref, g_ref, w_ref, qg_ref, kg_ref, bd_ref, q_ref, kv_ref):
    h = _rms(x_ref[...], g_ref[...]).astype(BF)
    bd = bd_ref[...]
    for g in range(N_GROUPS):
        c0 = g * 3 * ATTN_WIDTH
        q = _dot(h, w_ref[:, c0:c0 + ATTN_WIDTH])
        k = _dot(h, w_ref[:, c0 + ATTN_WIDTH:c0 + 2 * ATTN_WIDTH])
        v = _dot(h, w_ref[:, c0 + 2 * ATTN_WIDTH:c0 + 3 * ATTN_WIDTH])
        q_ref[:, g * ATTN_WIDTH:(g + 1) * ATTN_WIDTH] = _head_rms(q, qg_ref[...], bd)
        kv_ref[:, 2 * g * ATTN_WIDTH:(2 * g + 1) * ATTN_WIDTH] = _head_rms(k, kg_ref[...], bd)
        kv_ref[:, (2 * g + 1) * ATTN_WIDTH:(2 * g + 2) * ATTN_WIDTH] = v


def _qkv_sample(x, g, w_qkv, q_gain, k_gain, bd):
    n = x.shape[0]
    return pl.pallas_call(
        _qkv_sample_body,
        out_shape=[jax.ShapeDtypeStruct((n, N_GROUPS * ATTN_WIDTH), F32),
                   jax.ShapeDtypeStruct((n, N_GROUPS * 2 * ATTN_WIDTH), F32)],
        compiler_params=pltpu.CompilerParams(vmem_limit_bytes=VMEM_LIMIT),
        name="qkv_sample",
    )(x, g, w_qkv, q_gain, k_gain, bd)


def _attn_sample_body(*refs, n_alias):
    q_ref, kvn_ref = refs[0:2]
    c_refs, bt_refs, b0_ref = refs[2:5], refs[5:8], refs[8]
    outs = refs[9 + n_alias:]
    o_ref, n_refs = outs[0], outs[1:4]
    for h in range(N_HEADS):
        ms, ss, nums = [], [], []
        for g in range(N_GROUPS):
            c_ref, n_ref = c_refs[g], n_refs[g]
            keep = c_ref.shape[-1]
            last = lax.broadcasted_iota(jnp.int32, (HEAD_DIM, keep), 1) == keep - 1
            qc = q_ref[0, g, h]
            kn = kvn_ref[0, g, 0, h]
            vn = kvn_ref[0, g, 1, h]
            kt = c_ref[0, 0, 0, h]
            vt = c_ref[0, 0, 1, h]
            l = jnp.sum(kt * qc, axis=0, keepdims=True) + bt_refs[g][h]
            ln = jnp.sum(kn * qc, axis=0, keepdims=True) + b0_ref[g, h]
            m = jnp.maximum(jnp.max(l, axis=1, keepdims=True), ln)
            e = jnp.exp(l - m)
            en = jnp.exp(ln - m)
            ss.append(jnp.sum(e, axis=1, keepdims=True) + en)
            nums.append(jnp.sum(vt * e, axis=1, keepdims=True) + vn * en)
            ms.append(m)
            n_ref[0, 0, 0, h] = jnp.where(last, kn, pltpu.roll(kt, keep - 1, axis=1))
            n_ref[0, 0, 1, h] = jnp.where(last, vn, pltpu.roll(vt, keep - 1, axis=1))
        mm = jnp.maximum(jnp.maximum(ms[0], ms[1]), ms[2])
        w = [jnp.exp(m - mm) for m in ms]
        den = w[0] * ss[0] + w[1] * ss[1] + w[2] * ss[2]
        num = w[0] * nums[0] + w[1] * nums[1] + w[2] * nums[2]
        o_ref[0, h] = num / den


def _attn_sample(q, kvn, caches_t, layer, bts, b0, prev):
    n = q.shape[0]
    cache_specs = []
    for c, (win, dil) in zip(caches_t, GROUPS):
        assert c.shape[-1] == win, "the cache must hold one full window"
        cache_specs.append(pl.BlockSpec((1, 1) + c.shape[2:], lambda i: (layer, i, 0, 0, 0, 0)))
    n_alias = 0 if prev is None else N_GROUPS
    res = pl.pallas_call(
        functools.partial(_attn_sample_body, n_alias=n_alias),
        grid=(n,),
        in_specs=[
            pl.BlockSpec((1,) + q.shape[1:], lambda i: (i, 0, 0, 0, 0)),
            pl.BlockSpec((1,) + kvn.shape[1:], lambda i: (i, 0, 0, 0, 0, 0)),
            *cache_specs,
            *[_const_spec(bt.shape) for bt in bts],
            _const_spec(b0.shape),
            *[pl.BlockSpec(memory_space=pl.ANY)] * n_alias,
        ],
        out_specs=[pl.BlockSpec((1, N_HEADS, HEAD_DIM, 1), lambda i: (i, 0, 0, 0)), *cache_specs],
        out_shape=[jax.ShapeDtypeStruct((n, N_HEADS, HEAD_DIM, 1), F32)]
        + [jax.ShapeDtypeStruct(c.shape, c.dtype) for c in caches_t],
        input_output_aliases={9 + g: 1 + g for g in range(n_alias)},
        compiler_params=_cparams(("arbitrary",)),
        name="attn_sample",
    )(q, kvn, *caches_t, *bts, b0, *(prev or ()))
    return res[0], res[1:]


def _t5_bucket(dist):
    exact = N_BUCKETS // 2
    n = np.asarray(dist, dtype=np.float32)
    large = exact + np.log(np.maximum(n, 1.0) / exact) / math.log(MAX_DISTANCE / exact) * (N_BUCKETS - exact)
    large = np.minimum(np.floor(large), N_BUCKETS - 1)
    return np.where(n < exact, n, large).astype(np.int32)


def _bias_tables(rel_bias):
    ci = np.arange(2 * QBLK)[None, None, :]
    period = 3 * QBLK
    tabs, bts, b0s = [], [], []
    for g, (win, dil) in enumerate(GROUPS):
        bucket = _t5_bucket(dil * np.arange(N_KEYS + 1))
        bias = jnp.take(rel_bias, bucket, axis=0)[:, g * N_HEADS:(g + 1) * N_HEADS].T
        w = jnp.concatenate([bias[:, ::-1], jnp.full((N_HEADS, period - N_KEYS - 1), NEG, F32)], axis=1)
        band = jnp.tile(w, (1, QBLK))[:, :QBLK * (period - 1)]
        band = band.reshape(N_HEADS, QBLK, period - 1)[:, :, :2 * QBLK]
        band_first = jnp.where(ci >= QBLK, band, NEG)
        pair_rows = (N_HEADS // 2, 2 * QBLK, 2 * QBLK)
        tabs.append(jnp.stack([band.reshape(pair_rows), band_first.reshape(pair_rows)], axis=0))
        vals = bias[:, N_KEYS:0:-1][:, :, None]
        skipped = jnp.full((N_HEADS, N_KEYS, dil - 1), NEG, F32)
        bts.append(jnp.concatenate([vals, skipped], axis=2).reshape(N_HEADS, 1, win))
        b0s.append(bias[:, 0:1][:, :, None])
    return jnp.stack(tabs), bts, jnp.stack(b0s)


def kernel(x_prompt, x_sample, state_sc_conv, cache_kv_d1, cache_kv_d4, cache_kv_d16, state_ffn_conv,
           norm_mix, norm_ffn, sc_w_in, sc_conv_w, sc_w_out, attn_w_qkv, attn_q_norm, attn_k_norm,
           attn_w_out, rel_bias, ffn_w_gate, ffn_w_up, ffn_conv_w, ffn_conv_b, ffn_w_down):
    depth = norm_mix.shape[0]
    nb = x_sample.shape[0]
    caches_t = [jnp.transpose(c, (0, 1, 3, 4, 5, 2)) for c in (cache_kv_d1, cache_kv_d4, cache_kv_d16)]
    tables, bts, b0 = _bias_tables(rel_bias)
    seg = np.arange(ATTN_WIDTH) // HEAD_DIM
    bd = jnp.asarray(seg[:, None] == seg[None, :], BF)

    sc_w_in_b, sc_w_out_b = sc_w_in.astype(BF), sc_w_out.astype(BF)
    w_qkv_b, w_ao_b = attn_w_qkv.astype(BF), attn_w_out.astype(BF)
    w_gate_b, w_up_b, w_down_b = ffn_w_gate.astype(BF), ffn_w_up.astype(BF), ffn_w_down.astype(BF)

    xp = x_prompt
    xs = x_sample[:, 0, :]
    p_sc, s_sc, p_ffn, s_ffn = [], [], [], []
    p_kv = [[] for _ in range(N_GROUPS)]
    new_caches = None
    for i in range(depth):
        j = i // 2
        g_mix = norm_mix[i][None]
        g_ffn = norm_ffn[i][None]
        ffn_w = (g_ffn, w_gate_b[i], w_up_b[i], ffn_conv_w[i], ffn_conv_b[i][None], w_down_b[i])
        sf = state_ffn_conv[i]
        if i % 2 == 0:
            xp, hist = _mixer_prompt(xp, g_mix, sc_w_in_b[j], sc_conv_w[j], sc_w_out_b[j])
            p_sc.append(hist)
            st = state_sc_conv[j]
            xs, u = _mixer_sample(xs, g_mix, sc_w_in_b[j], sc_conv_w[j], sc_w_out_b[j], st[:, 0], st[:, 1])
            s_sc.append(jnp.stack([st[:, 1], u], axis=1))
            pre_p = pre_s = None
        else:
            q_gain = (jnp.tile(attn_q_norm[j], N_HEADS) * SCALE)[None]
            k_gain = jnp.tile(attn_k_norm[j], N_HEADS)[None]
            qkv, kv_rows = _qkv_prompt(xp, g_mix, w_qkv_b[j], q_gain, k_gain, bd)
            for g in range(N_GROUPS):
                p_kv[g].append(kv_rows[g].reshape(kv_rows[g].shape[:2] + (2, N_HEADS, HEAD_DIM)))
            pre_p = (_attn_prompt(qkv, tables), w_ao_b[j])

            qs, kvs = _qkv_sample(xs, g_mix, w_qkv_b[j], q_gain, k_gain, bd)
            qs = qs.reshape(nb, N_GROUPS, N_HEADS, HEAD_DIM, 1)
            kvs = kvs.reshape(nb, N_GROUPS, 2, N_HEADS, HEAD_DIM, 1)
            a_s, new_caches = _attn_sample(qs, kvs, caches_t, j, bts, b0, new_caches)
            pre_s = (a_s.reshape(nb, ATTN_WIDTH), w_ao_b[j])
        xp, fh = _ffn_prompt(xp, pre_p, *ffn_w)
        p_ffn.append(fh)
        xs, gate = _ffn_sample(xs, pre_s, *ffn_w, sf[:, 0], sf[:, 1])
        s_ffn.append(jnp.stack([sf[:, 1], gate], axis=1))

    s_kv = [jnp.transpose(c, (0, 1, 5, 2, 3, 4)) for c in new_caches]
    return (xp, xs[:, None, :],
            jnp.stack(p_sc, axis=0), jnp.stack(p_kv[0], axis=0), jnp.stack(p_kv[1], axis=0),
            jnp.stack(p_kv[2], axis=0), jnp.stack(p_ffn, axis=0),
            jnp.stack(s_sc, axis=0), s_kv[0], s_kv[1], s_kv[2], jnp.stack(s_ffn, axis=0))
```

```python
import functools
import math

import numpy as np
import jax
import jax.numpy as jnp
from jax import lax
from jax.experimental import pallas as pl
from jax.experimental.pallas import tpu as pltpu

D_MODEL = 1024
D_FF = 2816
N_HEADS = 8
HEAD_DIM = 64
ATTN_WIDTH = N_HEADS * HEAD_DIM
GROUPS = ((128, 1), (512, 4), (2048, 16))
N_GROUPS = len(GROUPS)
N_KEYS = 128
N_BUCKETS = 32
MAX_DISTANCE = 2048
EPS = 1e-6
NEG = -1e30
SCALE = HEAD_DIM ** -0.5
SUPER = 2048
QBLK = 128
LANES = 128
ATTN_UNROLL = 8
BF = jnp.bfloat16
F32 = jnp.float32

V7X_VMEM_BYTES = 64 * 1024 * 1024
VMEM_LIMIT = V7X_VMEM_BYTES - 8 * 1024 * 1024


def _cparams(sem):
    return pltpu.CompilerParams(dimension_semantics=sem, vmem_limit_bytes=VMEM_LIMIT)


def _const_spec(shape):
    nd = len(shape)
    return pl.BlockSpec(shape, lambda *_: (0,) * nd, pipeline_mode=pl.Buffered(1))


def _whole_spec(shape):
    nd = len(shape)
    return pl.BlockSpec(shape, lambda *_: (0,) * nd)


def _layer_spec(arr, layer):
    nd = arr.ndim
    return pl.BlockSpec((None,) + arr.shape[1:], lambda *_: (layer,) + (0,) * (nd - 1),
                        pipeline_mode=pl.Buffered(1))


def _dot(a, b):
    return jnp.dot(a, b, preferred_element_type=F32)


def _rms(x, g):
    r = lax.rsqrt(jnp.mean(x * x, axis=-1, keepdims=True) + EPS)
    return (x * r) * g


def _head_rms(x, gain, bd):
    ss = _dot((x * x).astype(BF), bd)
    r = lax.rsqrt(ss * (1.0 / HEAD_DIM) + EPS)
    return (x * r) * gain


def _conv3(buf, cw_ref, tm):
    out = cw_ref[0:1, :] * buf[6:6 + tm, :]
    out = out + cw_ref[1:2, :] * buf[7:7 + tm, :]
    return out + cw_ref[2:3, :] * buf[8:8 + tm, :]


def _mixer_body(x_ref, g_ref, win_ref, cw_ref, wout_ref, o_ref, hist_ref, ubuf, *, tm, nt):
    t = pl.program_id(1)

    @pl.when(t == 0)
    def _():
        ubuf[0:8, :] = jnp.zeros((8, D_MODEL), F32)

    x = x_ref[0]
    h = _rms(x, g_ref[...]).astype(BF)
    bg = _dot(h, win_ref[:, 0:D_MODEL])
    cg = _dot(h, win_ref[:, D_MODEL:2 * D_MODEL])
    xv = _dot(h, win_ref[:, 2 * D_MODEL:3 * D_MODEL])
    ubuf[8:8 + tm, :] = cg * xv
    conv = _conv3(ubuf, cw_ref, tm)
    y = _dot((bg * conv).astype(BF), wout_ref[...])
    o_ref[0] = x + y
    tail = ubuf[tm:tm + 8, :]
    ubuf[0:8, :] = tail

    @pl.when(t == nt - 1)
    def _():
        hist_ref[0] = tail[6:8, :]


def _mixer_prompt(x, layer, mix_layer, g, w_in, conv_w, w_out, tm=512):
    b, t, d = x.shape
    nt = t // tm
    return pl.pallas_call(
        functools.partial(_mixer_body, tm=tm, nt=nt),
        grid=(b, nt),
        in_specs=[
            pl.BlockSpec((1, tm, d), lambda i, j: (i, j, 0)),
            _layer_spec(g, layer),
            *[_layer_spec(a, mix_layer) for a in (w_in, conv_w, w_out)],
        ],
        out_specs=[
            pl.BlockSpec((1, tm, d), lambda i, j: (i, j, 0)),
            pl.BlockSpec((1, 2, d), lambda i, j: (i, 0, 0)),
        ],
        out_shape=[
            jax.ShapeDtypeStruct((b, t, d), F32),
            jax.ShapeDtypeStruct((b, 2, d), F32),
        ],
        scratch_shapes=[pltpu.VMEM((tm + 8, d), F32)],
        compiler_params=_cparams(("arbitrary", "arbitrary")),
        name="mixer_prompt",
    )(x, g, w_in, conv_w, w_out)


def _ffn_body(*refs, tm, nt, has_pre):
    if has_pre:
        x_ref, a_ref, wa_ref = refs[:3]
        refs = refs[3:]
    else:
        x_ref = refs[0]
        refs = refs[1:]
    g_ref, wg_ref, wu_ref, cw_ref, cb_ref, wd_ref, o_ref, hist_ref, gbuf = refs
    t = pl.program_id(1)

    @pl.when(t == 0)
    def _():
        gbuf[0:8, :] = jnp.zeros((8, D_FF), F32)

    x = x_ref[0]
    if has_pre:
        x = x + _dot(a_ref[0].astype(BF), wa_ref[...])
    h = _rms(x, g_ref[...]).astype(BF)
    gbuf[8:8 + tm, :] = _dot(h, wg_ref[...])
    up = _dot(h, wu_ref[...])
    z = _conv3(gbuf, cw_ref, tm) + cb_ref[...]
    act = (z * jax.nn.sigmoid(z)) * up
    o_ref[0] = x + _dot(act.astype(BF), wd_ref[...])
    tail = gbuf[tm:tm + 8, :]
    gbuf[0:8, :] = tail

    @pl.when(t == nt - 1)
    def _():
        hist_ref[0] = tail[6:8, :]


def _ffn_prompt(x, pre, layer, g, w_gate, w_up, conv_w, conv_b, w_down, tm=256):
    b, t, d = x.shape
    nt = t // tm
    row = lambda i, j: (i, j, 0)
    in_specs = [pl.BlockSpec((1, tm, d), row)]
    args = [x]
    if pre is not None:
        a, wa, pre_layer = pre
        in_specs += [pl.BlockSpec((1, tm, a.shape[-1]), row), _layer_spec(wa, pre_layer)]
        args += [a, wa]
    weights = (g, w_gate, w_up, conv_w, conv_b, w_down)
    in_specs += [_layer_spec(a, layer) for a in weights]
    args += weights
    return pl.pallas_call(
        functools.partial(_ffn_body, tm=tm, nt=nt, has_pre=pre is not None),
        grid=(b, nt),
        in_specs=in_specs,
        out_specs=[
            pl.BlockSpec((1, tm, d), row),
            pl.BlockSpec((1, 2, D_FF), lambda i, j: (i, 0, 0)),
        ],
        out_shape=[
            jax.ShapeDtypeStruct((b, t, d), F32),
            jax.ShapeDtypeStruct((b, 2, D_FF), F32),
        ],
        scratch_shapes=[pltpu.VMEM((tm + 8, D_FF), F32)],
        compiler_params=_cparams(("arbitrary", "arbitrary")),
        name="ffn_prompt",
    )(*args)


def _qkv_body(x_ref, g_ref, w_ref, qg_ref, kg_ref, bd_ref, *refs, tm, nt):
    qkv_refs = refs[:3 * N_GROUPS]
    cache_refs = refs[3 * N_GROUPS:4 * N_GROUPS]
    scr = refs[4 * N_GROUPS]
    t = pl.program_id(1)
    h = _rms(x_ref[0], g_ref[...]).astype(BF)
    bd = bd_ref[...]
    for g, (win, dil) in reversed(list(enumerate(GROUPS))):
        c0 = g * 3 * ATTN_WIDTH
        q = _dot(h, w_ref[:, c0:c0 + ATTN_WIDTH])
        k = _dot(h, w_ref[:, c0 + ATTN_WIDTH:c0 + 2 * ATTN_WIDTH])
        v = _dot(h, w_ref[:, c0 + 2 * ATTN_WIDTH:c0 + 3 * ATTN_WIDTH])
        q = _head_rms(q, qg_ref[...], bd)
        k = _head_rms(k, kg_ref[...], bd)

        c_ref = cache_refs[g]
        keep = win
        if keep >= tm:
            first = nt - keep // tm

            @pl.when(t >= first)
            def _():
                c_ref[0, :, 0:ATTN_WIDTH] = k
                c_ref[0, :, ATTN_WIDTH:2 * ATTN_WIDTH] = v
        else:
            @pl.when(t == nt - 1)
            def _():
                c_ref[0, :, 0:ATTN_WIDTH] = k[tm - keep:, :]
                c_ref[0, :, ATTN_WIDTH:2 * ATTN_WIDTH] = v[tm - keep:, :]

        outs = qkv_refs[3 * g:3 * g + 3]
        if dil == 1:
            for o_ref, val in zip(outs, (q, k, v)):
                o_ref[0, 0, 0] = val.astype(BF)
        else:
            rows = tm // dil
            chunks = ATTN_WIDTH // LANES
            for j, (o_ref, val) in enumerate(zip(outs, (q, k, v))):
                for cc in range(chunks):
                    scr[j * chunks + cc] = val[:, cc * LANES:(cc + 1) * LANES]
                for r in range(dil):
                    for cc in range(chunks):
                        o_ref[0, 0, r, :, cc * LANES:(cc + 1) * LANES] = (
                            scr[j * chunks + cc, pl.ds(r, rows, stride=dil), :].astype(BF))


def _qkv_prompt(x, layer, attn_layer, g, w_qkv, q_gain, k_gain, bd, tm=512):
    b, t, d = x.shape
    nt = t // tm
    tiles_per_super = SUPER // tm
    out_specs, out_shapes = [], []
    for win, dil in GROUPS:
        rows = tm // dil
        shp = (b, t // SUPER, dil, SUPER // dil, ATTN_WIDTH)
        spec = pl.BlockSpec(
            (1, 1, dil, rows, ATTN_WIDTH),
            lambda i, j: (i, j // tiles_per_super, 0, j % tiles_per_super, 0))
        out_specs += [spec] * 3
        out_shapes += [jax.ShapeDtypeStruct(shp, BF)] * 3
    for win, dil in GROUPS:
        keep = win
        if keep >= tm:
            first = nt - keep // tm
            spec = pl.BlockSpec((1, tm, 2 * ATTN_WIDTH),
                                lambda i, j, first=first: (i, jnp.maximum(j - first, 0), 0))
        else:
            spec = pl.BlockSpec((1, keep, 2 * ATTN_WIDTH), lambda i, j: (i, 0, 0))
        out_specs.append(spec)
        out_shapes.append(jax.ShapeDtypeStruct((b, keep, 2 * ATTN_WIDTH), F32))
    res = pl.pallas_call(
        functools.partial(_qkv_body, tm=tm, nt=nt),
        grid=(b, nt),
        in_specs=[
            pl.BlockSpec((1, tm, d), lambda i, j: (i, j, 0)),
            _layer_spec(g, layer),
            *[_layer_spec(a, attn_layer) for a in (w_qkv, q_gain, k_gain)],
            _const_spec((ATTN_WIDTH, ATTN_WIDTH)),
        ],
        out_specs=out_specs,
        out_shape=out_shapes,
        scratch_shapes=[pltpu.VMEM((3 * ATTN_WIDTH // LANES, tm, LANES), F32)],
        compiler_params=_cparams(("arbitrary", "arbitrary")),
        name="qkv_prompt",
    )(x, g, w_qkv, q_gain, k_gain, bd)
    qkv = [r.reshape(b, t, ATTN_WIDTH) for r in res[:3 * N_GROUPS]]
    return qkv, res[3 * N_GROUPS:]


def _attn_body(*refs):
    qkv_refs = refs[:3 * N_GROUPS]
    tb_ref = refs[3 * N_GROUPS]
    o_ref = refs[3 * N_GROUPS + 1]
    out_s, lse_s = refs[3 * N_GROUPS + 2:]
    n_super = o_ref.shape[1] // SUPER
    blocks = SUPER // QBLK
    lane = lax.broadcasted_iota(jnp.int32, (QBLK, LANES), 1)
    first_head = lane < HEAD_DIM
    contract_last = (((1,), (1,)), ((), ()))
    zero = jnp.zeros((), BF)
    lane2 = lax.broadcasted_iota(jnp.int32, (2 * QBLK, LANES), 1)
    first_head2 = lane2 < HEAD_DIM
    ones_h0 = jnp.where(first_head2, 1.0, 0.0).astype(BF)
    ones_h1 = jnp.where(first_head2, 0.0, 1.0).astype(BF)

    for s in range(n_super):
        for g, (win, dil) in enumerate(GROUPS):
            q_ref, k_ref, v_ref = qkv_refs[3 * g:3 * g + 3]
            per_res = blocks // dil

            def trip(it, carry, s=s, g=g, dil=dil, per_res=per_res,
                     q_ref=q_ref, k_ref=k_ref, v_ref=v_ref):
                def logits_of(u):
                    n = it * ATTN_UNROLL + u
                    c = n % per_res
                    r = n // per_res
                    start = s * SUPER + n * QBLK
                    if s == 0:
                        prev = jnp.where(c > 0, start - QBLK, start)
                        first = jnp.where(c == 0, 1, 0)
                    else:
                        prev = jnp.where(c > 0, start - QBLK, start - SUPER + SUPER // dil - QBLK)
                        first = 0
                    start = pl.multiple_of(start, QBLK)
                    prev = pl.multiple_of(prev, QBLK)
                    qb = q_ref[0, pl.ds(start, QBLK), :]
                    q2 = jnp.concatenate(
                        [jnp.where(first_head, qb, zero), jnp.where(first_head, zero, qb)], axis=0)
                    k2 = jnp.concatenate(
                        [k_ref[0, pl.ds(prev, QBLK), :], k_ref[0, pl.ds(start, QBLK), :]], axis=0)
                    logits = lax.dot_general(q2, k2, contract_last, preferred_element_type=F32)
                    return logits + tb_ref[g, first, 0], start, prev, r + dil * QBLK * c

                def probs_of(logits, start, prev, nat):
                    m = jnp.max(logits, axis=1, keepdims=True)
                    return jnp.exp(logits - m).astype(BF), m, start, prev, nat

                def finish(e, m, start, prev, nat):
                    v2 = jnp.concatenate(
                        [v_ref[0, pl.ds(prev, QBLK), :], v_ref[0, pl.ds(start, QBLK), :]], axis=0)
                    rhs = jnp.concatenate([
                        jnp.concatenate([jnp.where(first_head2, v2, zero), ones_h0], axis=1),
                        jnp.concatenate([jnp.where(first_head2, zero, v2), ones_h1], axis=1)], axis=0)
                    pv = _dot(jnp.concatenate([e[0:QBLK], e[QBLK:]], axis=1), rhs)
                    if dil == 1:
                        rows = pl.ds(pl.multiple_of(nat, QBLK), QBLK)
                    else:
                        rows = pl.ds(nat, QBLK, stride=dil)
                    row_sum = pv[:, LANES:]
                    out_s[g, rows, :] = pv[:, 0:LANES] / row_sum
                    lse_s[g, rows, :] = jnp.where(first_head, m[0:QBLK], m[QBLK:]) + jnp.log(row_sum)

                staged, probs = {}, {}
                for u in range(ATTN_UNROLL + 2):
                    if u < ATTN_UNROLL:
                        staged[u] = logits_of(u)
                    if 1 <= u <= ATTN_UNROLL:
                        probs[u - 1] = probs_of(*staged.pop(u - 1))
                    if u >= 2:
                        finish(*probs.pop(u - 2))
                return carry

            lax.fori_loop(0, blocks // ATTN_UNROLL, trip, 0)

        mrows = 256

        def merge(i, carry, s=s):
            rows = pl.ds(pl.multiple_of(i * mrows, mrows), mrows)
            l0, l1, l2 = lse_s[0, rows, :], lse_s[1, rows, :], lse_s[2, rows, :]
            top = jnp.maximum(jnp.maximum(l0, l1), l2)
            w0, w1, w2 = jnp.exp(l0 - top), jnp.exp(l1 - top), jnp.exp(l2 - top)
            num = w0 * out_s[0, rows, :] + w1 * out_s[1, rows, :] + w2 * out_s[2, rows, :]
            o_ref[0, pl.ds(pl.multiple_of(s * SUPER + i * mrows, mrows), mrows), :] = num / (w0 + w1 + w2)
            return carry

        lax.fori_loop(0, SUPER // mrows, merge, 0)


def _attn_prompt(qkv, tables):
    b, t, _ = qkv[0].shape
    pairs = N_HEADS // 2
    col = lambda i, j: (i, 0, j)
    return pl.pallas_call(
        _attn_body,
        grid=(b, pairs),
        in_specs=[pl.BlockSpec((1, t, LANES), col)] * (3 * N_GROUPS) + [
            pl.BlockSpec((N_GROUPS, 2, 1, 2 * QBLK, 2 * QBLK), lambda i, j: (0, 0, j, 0, 0)),
        ],
        out_specs=pl.BlockSpec((1, t, LANES), col),
        out_shape=jax.ShapeDtypeStruct((b, t, ATTN_WIDTH), F32),
        scratch_shapes=[pltpu.VMEM((N_GROUPS, SUPER, LANES), F32)] * 2,
        compiler_params=_cparams(("arbitrary", "arbitrary")),
        name="attn_prompt",
    )(*qkv, tables)


def _mixer_sample_body(x_ref, g_ref, win_ref, cw_ref, wout_ref, h0_ref, h1_ref, o_ref, u_ref):
    x = x_ref[...]
    h = _rms(x, g_ref[...]).astype(BF)
    bg = _dot(h, win_ref[:, 0:D_MODEL])
    cg = _dot(h, win_ref[:, D_MODEL:2 * D_MODEL])
    xv = _dot(h, win_ref[:, 2 * D_MODEL:3 * D_MODEL])
    u = cg * xv
    conv = cw_ref[0:1, :] * h0_ref[...]
    conv = conv + cw_ref[1:2, :] * h1_ref[...]
    conv = conv + cw_ref[2:3, :] * u
    o_ref[...] = x + _dot((bg * conv).astype(BF), wout_ref[...])
    u_ref[...] = u


def _mixer_sample(x, layer, mix_layer, g, w_in, conv_w, w_out, h0, h1):
    n, d = x.shape
    return pl.pallas_call(
        _mixer_sample_body,
        grid=(1,),
        in_specs=[_const_spec(x.shape), _layer_spec(g, layer),
                  *[_layer_spec(a, mix_layer) for a in (w_in, conv_w, w_out)],
                  _const_spec(h0.shape), _const_spec(h1.shape)],
        out_specs=[_whole_spec((n, d))] * 2,
        out_shape=[jax.ShapeDtypeStruct((n, d), F32)] * 2,
        compiler_params=_cparams(("arbitrary",)),
        name="mixer_sample",
    )(x, g, w_in, conv_w, w_out, h0, h1)


def _ffn_sample_body(*refs, has_pre):
    if has_pre:
        x_ref, a_ref, wa_ref = refs[:3]
        refs = refs[3:]
    else:
        x_ref = refs[0]
        refs = refs[1:]
    g_ref, wg_ref, wu_ref, cw_ref, cb_ref, wd_ref, h0_ref, h1_ref, o_ref, gate_ref = refs
    x = x_ref[...]
    if has_pre:
        x = x + _dot(a_ref[...].astype(BF), wa_ref[...])
    h = _rms(x, g_ref[...]).astype(BF)
    gate = _dot(h, wg_ref[...])
    up = _dot(h, wu_ref[...])
    z = cw_ref[0:1, :] * h0_ref[...]
    z = z + cw_ref[1:2, :] * h1_ref[...]
    z = z + cw_ref[2:3, :] * gate
    z = z + cb_ref[...]
    act = (z * jax.nn.sigmoid(z)) * up
    o_ref[...] = x + _dot(act.astype(BF), wd_ref[...])
    gate_ref[...] = gate


def _ffn_sample(x, pre, layer, g, w_gate, w_up, conv_w, conv_b, w_down, h0, h1):
    n, d = x.shape
    args, in_specs = [x], [_const_spec(x.shape)]
    if pre is not None:
        a, wa, pre_layer = pre
        args += [a, wa]
        in_specs += [_const_spec(a.shape), _layer_spec(wa, pre_layer)]
    weights = (g, w_gate, w_up, conv_w, conv_b, w_down)
    args += [*weights, h0, h1]
    in_specs += [*[_layer_spec(a, layer) for a in weights], _const_spec(h0.shape), _const_spec(h1.shape)]
    return pl.pallas_call(
        functools.partial(_ffn_sample_body, has_pre=pre is not None),
        grid=(1,),
        in_specs=in_specs,
        out_specs=[_whole_spec((n, d)), _whole_spec((n, D_FF))],
        out_shape=[jax.ShapeDtypeStruct((n, d), F32), jax.ShapeDtypeStruct((n, D_FF), F32)],
        compiler_params=_cparams(("arbitrary",)),
        name="ffn_sample",
    )(*args)


def _qkv_sample_body(x_ref, g_ref, w_ref, qg_ref, kg_ref, bd_ref, q_ref, kv_ref):
    h = _rms(x_ref[...], g_ref[...]).astype(BF)
    bd = bd_ref[...]
    for g in range(N_GROUPS):
        c0 = g * 3 * ATTN_WIDTH
        q = _dot(h, w_ref[:, c0:c0 + ATTN_WIDTH])
        k = _dot(h, w_ref[:, c0 + ATTN_WIDTH:c0 + 2 * ATTN_WIDTH])
        v = _dot(h, w_ref[:, c0 + 2 * ATTN_WIDTH:c0 + 3 * ATTN_WIDTH])
        q_ref[:, g * ATTN_WIDTH:(g + 1) * ATTN_WIDTH] = _head_rms(q, qg_ref[...], bd)
        kv_ref[:, 2 * g * ATTN_WIDTH:(2 * g + 1) * ATTN_WIDTH] = _head_rms(k, kg_ref[...], bd)
        kv_ref[:, (2 * g + 1) * ATTN_WIDTH:(2 * g + 2) * ATTN_WIDTH] = v


def _qkv_sample(x, layer, attn_layer, g, w_qkv, q_gain, k_gain, bd):
    n = x.shape[0]
    out_cols = (N_GROUPS * ATTN_WIDTH, N_GROUPS * 2 * ATTN_WIDTH)
    return pl.pallas_call(
        _qkv_sample_body,
        grid=(1,),
        in_specs=[_const_spec(x.shape), _layer_spec(g, layer),
                  *[_layer_spec(a, attn_layer) for a in (w_qkv, q_gain, k_gain)], _const_spec(bd.shape)],
        out_specs=[_whole_spec((n, c)) for c in out_cols],
        out_shape=[jax.ShapeDtypeStruct((n, c), F32) for c in out_cols],
        compiler_params=_cparams(("arbitrary",)),
        name="qkv_sample",
    )(x, g, w_qkv, q_gain, k_gain, bd)


def _attn_sample_body(*refs, n_alias):
    q_ref, kvn_ref = refs[0:2]
    c_refs, bt_refs, b0_ref = refs[2:5], refs[5:8], refs[8]
    outs = refs[9 + n_alias:]
    o_ref, n_refs = outs[0], outs[1:4]
    head_lane = lax.broadcasted_iota(jnp.int32, (HEAD_DIM, N_HEADS), 1)

    def head(h, out):
        mine = head_lane == h
        column = lambda x: jnp.sum(jnp.where(mine, x, 0.0), axis=1, keepdims=True)
        ms, ss, nums = [], [], []
        for g in range(N_GROUPS):
            c_ref, n_ref = c_refs[g], n_refs[g]
            keep = c_ref.shape[-1]
            last = lax.broadcasted_iota(jnp.int32, (HEAD_DIM, keep), 1) == keep - 1
            qc = column(q_ref[0, g])
            kn = column(kvn_ref[0, g, 0])
            vn = column(kvn_ref[0, g, 1])
            kt = c_ref[0, 0, 0, h]
            vt = c_ref[0, 0, 1, h]
            l = jnp.sum(kt * qc, axis=0, keepdims=True) + bt_refs[g][h]
            ln = jnp.sum(kn * qc, axis=0, keepdims=True) + b0_ref[g, h]
            m = jnp.maximum(jnp.max(l, axis=1, keepdims=True), ln)
            e = jnp.exp(l - m)
            en = jnp.exp(ln - m)
            ss.append(jnp.sum(e, axis=1, keepdims=True) + en)
            nums.append(jnp.sum(vt * e, axis=1, keepdims=True) + vn * en)
            ms.append(m)
            n_ref[0, 0, 0, h] = jnp.where(last, kn, pltpu.roll(kt, keep - 1, axis=1))
            n_ref[0, 0, 1, h] = jnp.where(last, vn, pltpu.roll(vt, keep - 1, axis=1))
        mm = jnp.maximum(jnp.maximum(ms[0], ms[1]), ms[2])
        w = [jnp.exp(m - mm) for m in ms]
        den = w[0] * ss[0] + w[1] * ss[1] + w[2] * ss[2]
        num = w[0] * nums[0] + w[1] * nums[1] + w[2] * nums[2]
        return jnp.where(mine, num / den, out)

    o_ref[0] = lax.fori_loop(0, N_HEADS, head, jnp.zeros((HEAD_DIM, N_HEADS), F32))


def _attn_sample(q, kvn, caches_t, layer, bts, b0, prev):
    n = q.shape[0]
    cache_specs = []
    for c, (win, dil) in zip(caches_t, GROUPS):
        assert c.shape[-1] == win, "the cache must hold one full window"
        cache_specs.append(pl.BlockSpec((1, 1) + c.shape[2:], lambda i: (layer, i, 0, 0, 0, 0)))
    n_alias = 0 if prev is None else N_GROUPS
    res = pl.pallas_call(
        functools.partial(_attn_sample_body, n_alias=n_alias),
        grid=(n,),
        in_specs=[
            pl.BlockSpec((1,) + q.shape[1:], lambda i: (i, 0, 0, 0)),
            pl.BlockSpec((1,) + kvn.shape[1:], lambda i: (i, 0, 0, 0, 0)),
            *cache_specs,
            *[_const_spec(bt.shape) for bt in bts],
            _const_spec(b0.shape),
            *[pl.BlockSpec(memory_space=pl.ANY)] * n_alias,
        ],
        out_specs=[pl.BlockSpec((1, HEAD_DIM, N_HEADS), lambda i: (i, 0, 0)), *cache_specs],
        out_shape=[jax.ShapeDtypeStruct((n, HEAD_DIM, N_HEADS), F32)]
        + [jax.ShapeDtypeStruct(c.shape, c.dtype) for c in caches_t],
        input_output_aliases={9 + g: 1 + g for g in range(n_alias)},
        compiler_params=_cparams(("arbitrary",)),
        name="attn_sample",
    )(q, kvn, *caches_t, *bts, b0, *(prev or ()))
    return res[0], res[1:]


def _t5_bucket(dist):
    exact = N_BUCKETS // 2
    n = np.asarray(dist, dtype=np.float32)
    large = exact + np.log(np.maximum(n, 1.0) / exact) / math.log(MAX_DISTANCE / exact) * (N_BUCKETS - exact)
    large = np.minimum(np.floor(large), N_BUCKETS - 1)
    return np.where(n < exact, n, large).astype(np.int32)


def _bias_tables(rel_bias):
    ci = np.arange(2 * QBLK)[None, None, :]
    period = 3 * QBLK
    tabs, bts, b0s = [], [], []
    for g, (win, dil) in enumerate(GROUPS):
        bucket = _t5_bucket(dil * np.arange(N_KEYS + 1))
        bias = jnp.take(rel_bias, bucket, axis=0)[:, g * N_HEADS:(g + 1) * N_HEADS].T
        w = jnp.concatenate([bias[:, ::-1], jnp.full((N_HEADS, period - N_KEYS - 1), NEG, F32)], axis=1)
        band = jnp.tile(w, (1, QBLK))[:, :QBLK * (period - 1)]
        band = band.reshape(N_HEADS, QBLK, period - 1)[:, :, :2 * QBLK]
        band_first = jnp.where(ci >= QBLK, band, NEG)
        pair_rows = (N_HEADS // 2, 2 * QBLK, 2 * QBLK)
        tabs.append(jnp.stack([band.reshape(pair_rows), band_first.reshape(pair_rows)], axis=0))
        vals = bias[:, N_KEYS:0:-1][:, :, None]
        skipped = jnp.full((N_HEADS, N_KEYS, dil - 1), NEG, F32)
        bts.append(jnp.concatenate([vals, skipped], axis=2).reshape(N_HEADS, 1, win))
        b0s.append(bias[:, 0:1][:, :, None])
    return jnp.stack(tabs), bts, jnp.stack(b0s)


def kernel(x_prompt, x_sample, state_sc_conv, cache_kv_d1, cache_kv_d4, cache_kv_d16, state_ffn_conv,
           norm_mix, norm_ffn, sc_w_in, sc_conv_w, sc_w_out, attn_w_qkv, attn_q_norm, attn_k_norm,
           attn_w_out, rel_bias, ffn_w_gate, ffn_w_up, ffn_conv_w, ffn_conv_b, ffn_w_down):
    depth = norm_mix.shape[0]
    nb = x_sample.shape[0]
    caches_t = [jnp.transpose(c, (0, 1, 3, 4, 5, 2)) for c in (cache_kv_d1, cache_kv_d4, cache_kv_d16)]
    tables, bts, b0 = _bias_tables(rel_bias)
    seg = np.arange(ATTN_WIDTH) // HEAD_DIM
    bd = jnp.asarray(seg[:, None] == seg[None, :], BF)

    mixer_w = (sc_w_in.astype(BF), sc_conv_w, sc_w_out.astype(BF))
    w_qkv_b, w_ao_b = attn_w_qkv.astype(BF), attn_w_out.astype(BF)
    ffn_w = (norm_ffn[:, None, :], ffn_w_gate.astype(BF), ffn_w_up.astype(BF), ffn_conv_w,
             ffn_conv_b[:, None, :], ffn_w_down.astype(BF))
    g_mix = norm_mix[:, None, :]
    q_gain = (jnp.tile(attn_q_norm, (1, N_HEADS)) * SCALE)[:, None, :]
    k_gain = jnp.tile(attn_k_norm, (1, N_HEADS))[:, None, :]

    xp = x_prompt
    xs = x_sample[:, 0, :]
    p_sc, s_sc, p_ffn, s_ffn = [], [], [], []
    p_kv = [[] for _ in range(N_GROUPS)]
    new_caches = None
    for i in range(depth):
        j = i // 2
        sf = state_ffn_conv[i]
        if i % 2 == 0:
            xp, hist = _mixer_prompt(xp, i, j, g_mix, *mixer_w)
            p_sc.append(hist)
            st = state_sc_conv[j]
            xs, u = _mixer_sample(xs, i, j, g_mix, *mixer_w, st[:, 0], st[:, 1])
            s_sc.append(jnp.stack([st[:, 1], u], axis=1))
            pre_p = pre_s = None
        else:
            qkv, kv_rows = _qkv_prompt(xp, i, j, g_mix, w_qkv_b, q_gain, k_gain, bd)
            for g in range(N_GROUPS):
                p_kv[g].append(kv_rows[g].reshape(kv_rows[g].shape[:2] + (2, N_HEADS, HEAD_DIM)))
            pre_p = (_attn_prompt(qkv, tables), w_ao_b, j)

            qs, kvs = _qkv_sample(xs, i, j, g_mix, w_qkv_b, q_gain, k_gain, bd)
            qs = jnp.swapaxes(qs.reshape(nb, N_GROUPS, N_HEADS, HEAD_DIM), -1, -2)
            kvs = jnp.swapaxes(kvs.reshape(nb, N_GROUPS, 2, N_HEADS, HEAD_DIM), -1, -2)
            a_s, new_caches = _attn_sample(qs, kvs, caches_t, j, bts, b0, new_caches)
            pre_s = (jnp.swapaxes(a_s, -1, -2).reshape(nb, ATTN_WIDTH), w_ao_b, j)
        xp, fh = _ffn_prompt(xp, pre_p, i, *ffn_w)
        p_ffn.append(fh)
        xs, gate = _ffn_sample(xs, pre_s, i, *ffn_w, sf[:, 0], sf[:, 1])
        s_ffn.append(jnp.stack([sf[:, 1], gate], axis=1))

    s_kv = [jnp.transpose(c, (0, 1, 5, 2, 3, 4)) for c in new_caches]
    return (xp, xs[:, None, :],
            jnp.stack(p_sc, axis=0), jnp.stack(p_kv[0], axis=0), jnp.stack(p_kv[1], axis=0),
            jnp.stack(p_kv[2], axis=0), jnp.stack(p_ffn, axis=0),
            jnp.stack(s_sc, axis=0), s_kv[0], s_kv[1], s_kv[2], jnp.stack(s_ffn, axis=0))
```

```python
import functools
import math

import numpy as np
import jax
import jax.numpy as jnp
from jax import lax
from jax.experimental import pallas as pl
from jax.experimental.pallas import tpu as pltpu

D_MODEL = 1024
D_FF = 2816
N_HEADS = 8
HEAD_DIM = 64
ATTN_WIDTH = N_HEADS * HEAD_DIM
GROUPS = ((128, 1), (512, 4), (2048, 16))
N_GROUPS = len(GROUPS)
N_KEYS = 128
N_BUCKETS = 32
MAX_DISTANCE = 2048
EPS = 1e-6
NEG = -1e30
SCALE = HEAD_DIM ** -0.5
SUPER = 2048
QBLK = 128
LANES = 128
ATTN_UNROLL = 8
BF = jnp.bfloat16
F32 = jnp.float32

V7X_VMEM_BYTES = 64 * 1024 * 1024
VMEM_LIMIT = V7X_VMEM_BYTES - 8 * 1024 * 1024


def _cparams(sem):
    return pltpu.CompilerParams(dimension_semantics=sem, vmem_limit_bytes=VMEM_LIMIT)


def _const_spec(shape):
    nd = len(shape)
    return pl.BlockSpec(shape, lambda *_: (0,) * nd, pipeline_mode=pl.Buffered(1))


def _whole_spec(shape):
    nd = len(shape)
    return pl.BlockSpec(shape, lambda *_: (0,) * nd)


def _layer_spec(arr, layer):
    nd = arr.ndim
    return pl.BlockSpec((None,) + arr.shape[1:], lambda *_: (layer,) + (0,) * (nd - 1),
                        pipeline_mode=pl.Buffered(1))


def _dot(a, b):
    return jnp.dot(a, b, preferred_element_type=F32)


def _rms(x, g):
    r = lax.rsqrt(jnp.mean(x * x, axis=-1, keepdims=True) + EPS)
    return (x * r) * g


def _head_rms(x, gain, bd):
    ss = _dot((x * x).astype(BF), bd)
    r = lax.rsqrt(ss * (1.0 / HEAD_DIM) + EPS)
    return (x * r) * gain


def _conv3(buf, cw_ref, tm):
    out = cw_ref[0:1, :] * buf[6:6 + tm, :]
    out = out + cw_ref[1:2, :] * buf[7:7 + tm, :]
    return out + cw_ref[2:3, :] * buf[8:8 + tm, :]


def _mixer_body(x_ref, g_ref, win_ref, cw_ref, wout_ref, o_ref, hist_ref, ubuf, *, tm, nt):
    t = pl.program_id(1)

    @pl.when(t == 0)
    def _():
        ubuf[0:8, :] = jnp.zeros((8, D_MODEL), F32)

    x = x_ref[0]
    h = _rms(x, g_ref[...]).astype(BF)
    bg = _dot(h, win_ref[:, 0:D_MODEL])
    cg = _dot(h, win_ref[:, D_MODEL:2 * D_MODEL])
    xv = _dot(h, win_ref[:, 2 * D_MODEL:3 * D_MODEL])
    ubuf[8:8 + tm, :] = cg * xv
    conv = _conv3(ubuf, cw_ref, tm)
    y = _dot((bg * conv).astype(BF), wout_ref[...])
    o_ref[0] = x + y
    tail = ubuf[tm:tm + 8, :]
    ubuf[0:8, :] = tail

    @pl.when(t == nt - 1)
    def _():
        hist_ref[0] = tail[6:8, :]


def _mixer_prompt(x, layer, mix_layer, g, w_in, conv_w, w_out, tm=512):
    b, t, d = x.shape
    nt = t // tm
    return pl.pallas_call(
        functools.partial(_mixer_body, tm=tm, nt=nt),
        grid=(b, nt),
        in_specs=[
            pl.BlockSpec((1, tm, d), lambda i, j: (i, j, 0)),
            _layer_spec(g, layer),
            *[_layer_spec(a, mix_layer) for a in (w_in, conv_w, w_out)],
        ],
        out_specs=[
            pl.BlockSpec((1, tm, d), lambda i, j: (i, j, 0)),
            pl.BlockSpec((1, 2, d), lambda i, j: (i, 0, 0)),
        ],
        out_shape=[
            jax.ShapeDtypeStruct((b, t, d), F32),
            jax.ShapeDtypeStruct((b, 2, d), F32),
        ],
        scratch_shapes=[pltpu.VMEM((tm + 8, d), F32)],
        compiler_params=_cparams(("arbitrary", "arbitrary")),
        name="mixer_prompt",
    )(x, g, w_in, conv_w, w_out)


def _sample_heads(first_head, q_ref, kvn_ref, c_refs, bt_refs, b0_ref, o_ref, n_refs):
    heads = o_ref.shape[-1]
    head_lane = lax.broadcasted_iota(jnp.int32, (HEAD_DIM, N_HEADS), 1)
    out_lane = lax.broadcasted_iota(jnp.int32, (HEAD_DIM, heads), 1)
    out = jnp.zeros((HEAD_DIM, heads), F32)
    for hh in range(heads):
        h = first_head + hh
        mine = head_lane == h
        column = lambda x: jnp.sum(jnp.where(mine, x, 0.0), axis=1, keepdims=True)
        ms, ss, nums = [], [], []
        for g in range(N_GROUPS):
            c_ref, n_ref = c_refs[g], n_refs[g]
            keep = c_ref.shape[-1]
            last = lax.broadcasted_iota(jnp.int32, (HEAD_DIM, keep), 1) == keep - 1
            qc = column(q_ref[0, g])
            kn = column(kvn_ref[0, g, 0])
            vn = column(kvn_ref[0, g, 1])
            kt = c_ref[0, 0, 0, hh]
            vt = c_ref[0, 0, 1, hh]
            l = jnp.sum(kt * qc, axis=0, keepdims=True) + bt_refs[g][h]
            ln = jnp.sum(kn * qc, axis=0, keepdims=True) + b0_ref[g, h]
            m = jnp.maximum(jnp.max(l, axis=1, keepdims=True), ln)
            e = jnp.exp(l - m)
            en = jnp.exp(ln - m)
            ss.append(jnp.sum(e, axis=1, keepdims=True) + en)
            nums.append(jnp.sum(vt * e, axis=1, keepdims=True) + vn * en)
            ms.append(m)
            n_ref[0, 0, 0, hh] = jnp.where(last, kn, pltpu.roll(kt, keep - 1, axis=1))
            n_ref[0, 0, 1, hh] = jnp.where(last, vn, pltpu.roll(vt, keep - 1, axis=1))
        mm = jnp.maximum(jnp.maximum(ms[0], ms[1]), ms[2])
        w = [jnp.exp(m - mm) for m in ms]
        den = w[0] * ss[0] + w[1] * ss[1] + w[2] * ss[2]
        num = w[0] * nums[0] + w[1] * nums[1] + w[2] * nums[2]
        out = jnp.where(out_lane == hh, num / den, out)
    o_ref[0] = out


N_SAMPLE_IN = 9


def _ffn_body(*refs, tm, nt, has_pre, n_sample_in, n_alias):
    if has_pre:
        x_ref, a_ref, wa_ref = refs[:3]
        refs = refs[3:]
    else:
        x_ref = refs[0]
        refs = refs[1:]
    g_ref, wg_ref, wu_ref, cw_ref, cb_ref, wd_ref = refs[:6]
    sample_in = refs[6:6 + n_sample_in]
    outs = refs[6 + n_sample_in + n_alias:-1]
    o_ref, hist_ref, sample_out = outs[0], outs[1], outs[2:]
    gbuf = refs[-1]
    t = pl.program_id(1)

    @pl.when(t == 0)
    def _():
        gbuf[0:8, :] = jnp.zeros((8, D_FF), F32)

    x = x_ref[0]
    if has_pre:
        x = x + _dot(a_ref[0].astype(BF), wa_ref[...])
    h = _rms(x, g_ref[...]).astype(BF)
    gbuf[8:8 + tm, :] = _dot(h, wg_ref[...])
    up = _dot(h, wu_ref[...])
    if n_sample_in:
        q_ref, kvn_ref = sample_in[0:2]
        heads = sample_out[0].shape[-1]
        step = pl.program_id(0) * nt + t
        _sample_heads((step % (N_HEADS // heads)) * heads, q_ref, kvn_ref, sample_in[2:5], sample_in[5:8],
                      sample_in[8], sample_out[0], sample_out[1:4])
    z = _conv3(gbuf, cw_ref, tm) + cb_ref[...]
    act = (z * jax.nn.sigmoid(z)) * up
    o_ref[0] = x + _dot(act.astype(BF), wd_ref[...])
    tail = gbuf[tm:tm + 8, :]
    gbuf[0:8, :] = tail

    @pl.when(t == nt - 1)
    def _():
        hist_ref[0] = tail[6:8, :]


def _ffn_prompt(x, pre, layer, g, w_gate, w_up, conv_w, conv_b, w_down, sample=None, tm=256):
    b, t, d = x.shape
    nt = t // tm
    row = lambda i, j: (i, j, 0)
    in_specs = [pl.BlockSpec((1, tm, d), row)]
    args = [x]
    if pre is not None:
        a, wa, pre_layer = pre
        in_specs += [pl.BlockSpec((1, tm, a.shape[-1]), row), _layer_spec(wa, pre_layer)]
        args += [a, wa]
    weights = (g, w_gate, w_up, conv_w, conv_b, w_down)
    in_specs += [_layer_spec(a, layer) for a in weights]
    args += weights
    out_specs = [
        pl.BlockSpec((1, tm, d), row),
        pl.BlockSpec((1, 2, D_FF), lambda i, j: (i, 0, 0)),
    ]
    out_shape = [
        jax.ShapeDtypeStruct((b, t, d), F32),
        jax.ShapeDtypeStruct((b, 2, D_FF), F32),
    ]
    aliases = {}
    if sample is not None:
        q, kvn, caches_t, attn_layer, bts, b0, prev = sample
        nb = q.shape[0]
        steps = b * nt
        parts = steps // nb
        heads = N_HEADS // parts
        assert parts * nb == steps and heads * parts == N_HEADS
        cache_specs = []
        for c, (win, dil) in zip(caches_t, GROUPS):
            assert c.shape[-1] == win, "the cache must hold one full window"
            cache_specs.append(pl.BlockSpec(
                (1, 1, 2, heads, HEAD_DIM, win),
                lambda i, j: (attn_layer, (i * nt + j) // parts, 0, (i * nt + j) % parts, 0, 0)))
        in_specs += [
            pl.BlockSpec((1,) + q.shape[1:], lambda i, j: ((i * nt + j) // parts, 0, 0, 0)),
            pl.BlockSpec((1,) + kvn.shape[1:], lambda i, j: ((i * nt + j) // parts, 0, 0, 0, 0)),
            *cache_specs,
            *[_const_spec(bt.shape) for bt in bts],
            _const_spec(b0.shape),
        ]
        args += [q, kvn, *caches_t, *bts, b0]
        if prev is not None:
            in_specs += [pl.BlockSpec(memory_space=pl.ANY)] * N_GROUPS
            aliases = {len(args) + k: 3 + k for k in range(N_GROUPS)}
            args += list(prev)
        out_specs += [pl.BlockSpec((1, HEAD_DIM, heads), lambda i, j: (i * nt + j, 0, 0)), *cache_specs]
        out_shape += [jax.ShapeDtypeStruct((steps, HEAD_DIM, heads), F32)]
        out_shape += [jax.ShapeDtypeStruct(c.shape, c.dtype) for c in caches_t]
    return pl.pallas_call(
        functools.partial(_ffn_body, tm=tm, nt=nt, has_pre=pre is not None,
                          n_sample_in=0 if sample is None else N_SAMPLE_IN, n_alias=len(aliases)),
        grid=(b, nt),
        in_specs=in_specs,
        out_specs=out_specs,
        out_shape=out_shape,
        input_output_aliases=aliases,
        scratch_shapes=[pltpu.VMEM((tm + 8, D_FF), F32)],
        compiler_params=_cparams(("arbitrary", "arbitrary")),
        name="ffn_prompt",
    )(*args)


def _qkv_body(x_ref, g_ref, w_ref, qg_ref, kg_ref, bd_ref, *refs, tm, nt):
    qkv_refs = refs[:3 * N_GROUPS]
    cache_refs = refs[3 * N_GROUPS:4 * N_GROUPS]
    t = pl.program_id(1)
    h = _rms(x_ref[0], g_ref[...]).astype(BF)
    bd = bd_ref[...]
    for g, (win, dil) in reversed(list(enumerate(GROUPS))):
        c0 = g * 3 * ATTN_WIDTH
        q = _dot(h, w_ref[:, c0:c0 + ATTN_WIDTH])
        k = _dot(h, w_ref[:, c0 + ATTN_WIDTH:c0 + 2 * ATTN_WIDTH])
        v = _dot(h, w_ref[:, c0 + 2 * ATTN_WIDTH:c0 + 3 * ATTN_WIDTH])
        q = _head_rms(q, qg_ref[...], bd)
        k = _head_rms(k, kg_ref[...], bd)

        c_ref = cache_refs[g]
        keep = win
        if keep >= tm:
            first = nt - keep // tm

            @pl.when(t >= first)
            def _():
                c_ref[0, :, 0:ATTN_WIDTH] = k
                c_ref[0, :, ATTN_WIDTH:2 * ATTN_WIDTH] = v
        else:
            @pl.when(t == nt - 1)
            def _():
                c_ref[0, :, 0:ATTN_WIDTH] = k[tm - keep:, :]
                c_ref[0, :, ATTN_WIDTH:2 * ATTN_WIDTH] = v[tm - keep:, :]

        for o_ref, val in zip(qkv_refs[3 * g:3 * g + 3], (q, k, v)):
            if dil == 1:
                o_ref[0, 0, 0] = val.astype(BF)
            else:
                by_residue = pltpu.einshape("abc->bac", val.reshape(tm // dil, dil, ATTN_WIDTH))
                o_ref[0, 0] = by_residue.astype(BF)


def _qkv_prompt(x, layer, attn_layer, g, w_qkv, q_gain, k_gain, bd, tm=512):
    b, t, d = x.shape
    nt = t // tm
    tiles_per_super = SUPER // tm
    out_specs, out_shapes = [], []
    for win, dil in GROUPS:
        rows = tm // dil
        shp = (b, t // SUPER, dil, SUPER // dil, ATTN_WIDTH)
        spec = pl.BlockSpec(
            (1, 1, dil, rows, ATTN_WIDTH),
            lambda i, j: (i, j // tiles_per_super, 0, j % tiles_per_super, 0))
        out_specs += [spec] * 3
        out_shapes += [jax.ShapeDtypeStruct(shp, BF)] * 3
    for win, dil in GROUPS:
        keep = win
        if keep >= tm:
            first = nt - keep // tm
            spec = pl.BlockSpec((1, tm, 2 * ATTN_WIDTH),
                                lambda i, j, first=first: (i, jnp.maximum(j - first, 0), 0))
        else:
            spec = pl.BlockSpec((1, keep, 2 * ATTN_WIDTH), lambda i, j: (i, 0, 0))
        out_specs.append(spec)
        out_shapes.append(jax.ShapeDtypeStruct((b, keep, 2 * ATTN_WIDTH), F32))
    res = pl.pallas_call(
        functools.partial(_qkv_body, tm=tm, nt=nt),
        grid=(b, nt),
        in_specs=[
            pl.BlockSpec((1, tm, d), lambda i, j: (i, j, 0)),
            _layer_spec(g, layer),
            *[_layer_spec(a, attn_layer) for a in (w_qkv, q_gain, k_gain)],
            _const_spec((ATTN_WIDTH, ATTN_WIDTH)),
        ],
        out_specs=out_specs,
        out_shape=out_shapes,
        compiler_params=_cparams(("arbitrary", "arbitrary")),
        name="qkv_prompt",
    )(x, g, w_qkv, q_gain, k_gain, bd)
    qkv = [r.reshape(b, t, ATTN_WIDTH) for r in res[:3 * N_GROUPS]]
    return qkv, res[3 * N_GROUPS:]


def _attn_body(*refs):
    qkv_refs = refs[:3 * N_GROUPS]
    tb_ref = refs[3 * N_GROUPS]
    o_ref = refs[3 * N_GROUPS + 1]
    out_s, lse_s = refs[3 * N_GROUPS + 2:]
    n_super = o_ref.shape[1] // SUPER
    blocks = SUPER // QBLK
    lane = lax.broadcasted_iota(jnp.int32, (QBLK, LANES), 1)
    first_head = lane < HEAD_DIM
    contract_last = (((1,), (1,)), ((), ()))
    zero = jnp.zeros((), BF)
    lane2 = lax.broadcasted_iota(jnp.int32, (2 * QBLK, LANES), 1)
    first_head2 = lane2 < HEAD_DIM
    ones_h0 = jnp.where(first_head2, 1.0, 0.0).astype(BF)
    ones_h1 = jnp.where(first_head2, 0.0, 1.0).astype(BF)

    for s in range(n_super):
        for g, (win, dil) in enumerate(GROUPS):
            q_ref, k_ref, v_ref = qkv_refs[3 * g:3 * g + 3]
            per_res = blocks // dil

            def trip(it, carry, s=s, g=g, dil=dil, per_res=per_res,
                     q_ref=q_ref, k_ref=k_ref, v_ref=v_ref):
                def logits_of(u):
                    n = it * ATTN_UNROLL + u
                    c = n % per_res
                    r = n // per_res
                    start = s * SUPER + n * QBLK
                    if s == 0:
                        prev = jnp.where(c > 0, start - QBLK, start)
                        first = jnp.where(c == 0, 1, 0)
                    else:
                        prev = jnp.where(c > 0, start - QBLK, start - SUPER + SUPER // dil - QBLK)
                        first = 0
                    start = pl.multiple_of(start, QBLK)
                    prev = pl.multiple_of(prev, QBLK)
                    qb = q_ref[0, pl.ds(start, QBLK), :]
                    q2 = jnp.concatenate(
                        [jnp.where(first_head, qb, zero), jnp.where(first_head, zero, qb)], axis=0)
                    k2 = jnp.concatenate(
                        [k_ref[0, pl.ds(prev, QBLK), :], k_ref[0, pl.ds(start, QBLK), :]], axis=0)
                    logits = lax.dot_general(q2, k2, contract_last, preferred_element_type=F32)
                    return logits + tb_ref[g, first, 0], start, prev, r + dil * QBLK * c

                def probs_of(logits, start, prev, nat):
                    m = jnp.max(logits, axis=1, keepdims=True)
                    return jnp.exp(logits - m).astype(BF), m, start, prev, nat

                def finish(e, m, start, prev, nat):
                    v2 = jnp.concatenate(
                        [v_ref[0, pl.ds(prev, QBLK), :], v_ref[0, pl.ds(start, QBLK), :]], axis=0)
                    rhs = jnp.concatenate([
                        jnp.concatenate([jnp.where(first_head2, v2, zero), ones_h0], axis=1),
                        jnp.concatenate([jnp.where(first_head2, zero, v2), ones_h1], axis=1)], axis=0)
                    pv = _dot(jnp.concatenate([e[0:QBLK], e[QBLK:]], axis=1), rhs)
                    if dil == 1:
                        rows = pl.ds(pl.multiple_of(nat, QBLK), QBLK)
                    else:
                        rows = pl.ds(nat, QBLK, stride=dil)
                    row_sum = pv[:, LANES:]
                    out_s[g, rows, :] = pv[:, 0:LANES] / row_sum
                    lse_s[g, rows, :] = jnp.where(first_head, m[0:QBLK], m[QBLK:]) + jnp.log(row_sum)

                staged, probs = {}, {}
                for u in range(ATTN_UNROLL + 2):
                    if u < ATTN_UNROLL:
                        staged[u] = logits_of(u)
                    if 1 <= u <= ATTN_UNROLL:
                        probs[u - 1] = probs_of(*staged.pop(u - 1))
                    if u >= 2:
                        finish(*probs.pop(u - 2))
                return carry

            lax.fori_loop(0, blocks // ATTN_UNROLL, trip, 0)

        mrows = 256

        def merge(i, carry, s=s):
            rows = pl.ds(pl.multiple_of(i * mrows, mrows), mrows)
            l0, l1, l2 = lse_s[0, rows, :], lse_s[1, rows, :], lse_s[2, rows, :]
            top = jnp.maximum(jnp.maximum(l0, l1), l2)
            w0, w1, w2 = jnp.exp(l0 - top), jnp.exp(l1 - top), jnp.exp(l2 - top)
            num = w0 * out_s[0, rows, :] + w1 * out_s[1, rows, :] + w2 * out_s[2, rows, :]
            o_ref[0, pl.ds(pl.multiple_of(s * SUPER + i * mrows, mrows), mrows), :] = num / (w0 + w1 + w2)
            return carry

        lax.fori_loop(0, SUPER // mrows, merge, 0)


def _attn_prompt(qkv, tables):
    b, t, _ = qkv[0].shape
    pairs = N_HEADS // 2
    col = lambda i, j: (i, 0, j)
    return pl.pallas_call(
        _attn_body,
        grid=(b, pairs),
        in_specs=[pl.BlockSpec((1, t, LANES), col)] * (3 * N_GROUPS) + [
            pl.BlockSpec((N_GROUPS, 2, 1, 2 * QBLK, 2 * QBLK), lambda i, j: (0, 0, j, 0, 0)),
        ],
        out_specs=pl.BlockSpec((1, t, LANES), col),
        out_shape=jax.ShapeDtypeStruct((b, t, ATTN_WIDTH), F32),
        scratch_shapes=[pltpu.VMEM((N_GROUPS, SUPER, LANES), F32)] * 2,
        compiler_params=_cparams(("arbitrary", "arbitrary")),
        name="attn_prompt",
    )(*qkv, tables)


def _mixer_sample_body(x_ref, g_ref, win_ref, cw_ref, wout_ref, h0_ref, h1_ref, o_ref, u_ref):
    x = x_ref[...]
    h = _rms(x, g_ref[...]).astype(BF)
    bg = _dot(h, win_ref[:, 0:D_MODEL])
    cg = _dot(h, win_ref[:, D_MODEL:2 * D_MODEL])
    xv = _dot(h, win_ref[:, 2 * D_MODEL:3 * D_MODEL])
    u = cg * xv
    conv = cw_ref[0:1, :] * h0_ref[...]
    conv = conv + cw_ref[1:2, :] * h1_ref[...]
    conv = conv + cw_ref[2:3, :] * u
    o_ref[...] = x + _dot((bg * conv).astype(BF), wout_ref[...])
    u_ref[...] = u


def _mixer_sample(x, layer, mix_layer, g, w_in, conv_w, w_out, h0, h1):
    n, d = x.shape
    return pl.pallas_call(
        _mixer_sample_body,
        grid=(1,),
        in_specs=[_const_spec(x.shape), _layer_spec(g, layer),
                  *[_layer_spec(a, mix_layer) for a in (w_in, conv_w, w_out)],
                  _const_spec(h0.shape), _const_spec(h1.shape)],
        out_specs=[_whole_spec((n, d))] * 2,
        out_shape=[jax.ShapeDtypeStruct((n, d), F32)] * 2,
        compiler_params=_cparams(("arbitrary",)),
        name="mixer_sample",
    )(x, g, w_in, conv_w, w_out, h0, h1)


def _ffn_sample_body(*refs, has_pre):
    if has_pre:
        x_ref, a_ref, wa_ref = refs[:3]
        refs = refs[3:]
    else:
        x_ref = refs[0]
        refs = refs[1:]
    g_ref, wg_ref, wu_ref, cw_ref, cb_ref, wd_ref, h0_ref, h1_ref, o_ref, gate_ref = refs
    x = x_ref[...]
    if has_pre:
        x = x + _dot(a_ref[...].astype(BF), wa_ref[...])
    h = _rms(x, g_ref[...]).astype(BF)
    gate = _dot(h, wg_ref[...])
    up = _dot(h, wu_ref[...])
    z = cw_ref[0:1, :] * h0_ref[...]
    z = z + cw_ref[1:2, :] * h1_ref[...]
    z = z + cw_ref[2:3, :] * gate
    z = z + cb_ref[...]
    act = (z * jax.nn.sigmoid(z)) * up
    o_ref[...] = x + _dot(act.astype(BF), wd_ref[...])
    gate_ref[...] = gate


def _ffn_sample(x, pre, layer, g, w_gate, w_up, conv_w, conv_b, w_down, h0, h1):
    n, d = x.shape
    args, in_specs = [x], [_const_spec(x.shape)]
    if pre is not None:
        a, wa, pre_layer = pre
        args += [a, wa]
        in_specs += [_const_spec(a.shape), _layer_spec(wa, pre_layer)]
    weights = (g, w_gate, w_up, conv_w, conv_b, w_down)
    args += [*weights, h0, h1]
    in_specs += [*[_layer_spec(a, layer) for a in weights], _const_spec(h0.shape), _const_spec(h1.shape)]
    return pl.pallas_call(
        functools.partial(_ffn_sample_body, has_pre=pre is not None),
        grid=(1,),
        in_specs=in_specs,
        out_specs=[_whole_spec((n, d)), _whole_spec((n, D_FF))],
        out_shape=[jax.ShapeDtypeStruct((n, d), F32), jax.ShapeDtypeStruct((n, D_FF), F32)],
        compiler_params=_cparams(("arbitrary",)),
        name="ffn_sample",
    )(*args)


def _qkv_sample_body(x_ref, g_ref, w_ref, qg_ref, kg_ref, bd_ref, q_ref, kv_ref):
    h = _rms(x_ref[...], g_ref[...]).astype(BF)
    bd = bd_ref[...]
    for g in range(N_GROUPS):
        c0 = g * 3 * ATTN_WIDTH
        q = _dot(h, w_ref[:, c0:c0 + ATTN_WIDTH])
        k = _dot(h, w_ref[:, c0 + ATTN_WIDTH:c0 + 2 * ATTN_WIDTH])
        v = _dot(h, w_ref[:, c0 + 2 * ATTN_WIDTH:c0 + 3 * ATTN_WIDTH])
        q_ref[:, g * ATTN_WIDTH:(g + 1) * ATTN_WIDTH] = _head_rms(q, qg_ref[...], bd)
        kv_ref[:, 2 * g * ATTN_WIDTH:(2 * g + 1) * ATTN_WIDTH] = _head_rms(k, kg_ref[...], bd)
        kv_ref[:, (2 * g + 1) * ATTN_WIDTH:(2 * g + 2) * ATTN_WIDTH] = v


def _qkv_sample(x, layer, attn_layer, g, w_qkv, q_gain, k_gain, bd):
    n = x.shape[0]
    out_cols = (N_GROUPS * ATTN_WIDTH, N_GROUPS * 2 * ATTN_WIDTH)
    return pl.pallas_call(
        _qkv_sample_body,
        grid=(1,),
        in_specs=[_const_spec(x.shape), _layer_spec(g, layer),
                  *[_layer_spec(a, attn_layer) for a in (w_qkv, q_gain, k_gain)], _const_spec(bd.shape)],
        out_specs=[_whole_spec((n, c)) for c in out_cols],
        out_shape=[jax.ShapeDtypeStruct((n, c), F32) for c in out_cols],
        compiler_params=_cparams(("arbitrary",)),
        name="qkv_sample",
    )(x, g, w_qkv, q_gain, k_gain, bd)


def _t5_bucket(dist):
    exact = N_BUCKETS // 2
    n = np.asarray(dist, dtype=np.float32)
    large = exact + np.log(np.maximum(n, 1.0) / exact) / math.log(MAX_DISTANCE / exact) * (N_BUCKETS - exact)
    large = np.minimum(np.floor(large), N_BUCKETS - 1)
    return np.where(n < exact, n, large).astype(np.int32)


def _bias_tables(rel_bias):
    ci = np.arange(2 * QBLK)[None, None, :]
    period = 3 * QBLK
    tabs, bts, b0s = [], [], []
    for g, (win, dil) in enumerate(GROUPS):
        bucket = _t5_bucket(dil * np.arange(N_KEYS + 1))
        bias = jnp.take(rel_bias, bucket, axis=0)[:, g * N_HEADS:(g + 1) * N_HEADS].T
        w = jnp.concatenate([bias[:, ::-1], jnp.full((N_HEADS, period - N_KEYS - 1), NEG, F32)], axis=1)
        band = jnp.tile(w, (1, QBLK))[:, :QBLK * (period - 1)]
        band = band.reshape(N_HEADS, QBLK, period - 1)[:, :, :2 * QBLK]
        band_first = jnp.where(ci >= QBLK, band, NEG)
        pair_rows = (N_HEADS // 2, 2 * QBLK, 2 * QBLK)
        tabs.append(jnp.stack([band.reshape(pair_rows), band_first.reshape(pair_rows)], axis=0))
        vals = bias[:, N_KEYS:0:-1][:, :, None]
        skipped = jnp.full((N_HEADS, N_KEYS, dil - 1), NEG, F32)
        bts.append(jnp.concatenate([vals, skipped], axis=2).reshape(N_HEADS, 1, win))
        b0s.append(bias[:, 0:1][:, :, None])
    return jnp.stack(tabs), bts, jnp.stack(b0s)


def kernel(x_prompt, x_sample, state_sc_conv, cache_kv_d1, cache_kv_d4, cache_kv_d16, state_ffn_conv,
           norm_mix, norm_ffn, sc_w_in, sc_conv_w, sc_w_out, attn_w_qkv, attn_q_norm, attn_k_norm,
           attn_w_out, rel_bias, ffn_w_gate, ffn_w_up, ffn_conv_w, ffn_conv_b, ffn_w_down):
    depth = norm_mix.shape[0]
    nb = x_sample.shape[0]
    caches_t = [jnp.transpose(c, (0, 1, 3, 4, 5, 2)) for c in (cache_kv_d1, cache_kv_d4, cache_kv_d16)]
    tables, bts, b0 = _bias_tables(rel_bias)
    seg = np.arange(ATTN_WIDTH) // HEAD_DIM
    bd = jnp.asarray(seg[:, None] == seg[None, :], BF)

    mixer_w = (sc_w_in.astype(BF), sc_conv_w, sc_w_out.astype(BF))
    w_qkv_b, w_ao_b = attn_w_qkv.astype(BF), attn_w_out.astype(BF)
    ffn_w = (norm_ffn[:, None, :], ffn_w_gate.astype(BF), ffn_w_up.astype(BF), ffn_conv_w,
             ffn_conv_b[:, None, :], ffn_w_down.astype(BF))
    g_mix = norm_mix[:, None, :]
    q_gain = (jnp.tile(attn_q_norm, (1, N_HEADS)) * SCALE)[:, None, :]
    k_gain = jnp.tile(attn_k_norm, (1, N_HEADS))[:, None, :]

    xp = x_prompt
    xs = x_sample[:, 0, :]
    p_sc, s_sc, p_ffn, s_ffn = [], [], [], []
    p_kv = [[] for _ in range(N_GROUPS)]
    new_caches = None
    for i in range(depth):
        j = i // 2
        sf = state_ffn_conv[i]
        if i % 2 == 0:
            xp, hist = _mixer_prompt(xp, i, j, g_mix, *mixer_w)
            p_sc.append(hist)
            st = state_sc_conv[j]
            xs, u = _mixer_sample(xs, i, j, g_mix, *mixer_w, st[:, 0], st[:, 1])
            s_sc.append(jnp.stack([st[:, 1], u], axis=1))
            pre_p = pre_s = None
        else:
            qkv, kv_rows = _qkv_prompt(xp, i, j, g_mix, w_qkv_b, q_gain, k_gain, bd)
            for g in range(N_GROUPS):
                p_kv[g].append(kv_rows[g].reshape(kv_rows[g].shape[:2] + (2, N_HEADS, HEAD_DIM)))
            pre_p = (_attn_prompt(qkv, tables), w_ao_b, j)

            qs, kvs = _qkv_sample(xs, i, j, g_mix, w_qkv_b, q_gain, k_gain, bd)
            qs = jnp.swapaxes(qs.reshape(nb, N_GROUPS, N_HEADS, HEAD_DIM), -1, -2)
            kvs = jnp.swapaxes(kvs.reshape(nb, N_GROUPS, 2, N_HEADS, HEAD_DIM), -1, -2)
            xp, fh, a_s, *new_caches = _ffn_prompt(
                xp, pre_p, i, *ffn_w, sample=(qs, kvs, caches_t, j, bts, b0, new_caches))
            a_s = jnp.swapaxes(a_s.reshape(nb, -1, HEAD_DIM, a_s.shape[-1]), -1, -2)
            pre_s = (a_s.reshape(nb, ATTN_WIDTH), w_ao_b, j)
        if i % 2 == 0:
            xp, fh = _ffn_prompt(xp, pre_p, i, *ffn_w)
        p_ffn.append(fh)
        xs, gate = _ffn_sample(xs, pre_s, i, *ffn_w, sf[:, 0], sf[:, 1])
        s_ffn.append(jnp.stack([sf[:, 1], gate], axis=1))

    s_kv = [jnp.transpose(c, (0, 1, 5, 2, 3, 4)) for c in new_caches]
    return (xp, xs[:, None, :],
            jnp.stack(p_sc, axis=0), jnp.stack(p_kv[0], axis=0), jnp.stack(p_kv[1], axis=0),
            jnp.stack(p_kv[2], axis=0), jnp.stack(p_ffn, axis=0),
            jnp.stack(s_sc, axis=0), s_kv[0], s_kv[1], s_kv[2], jnp.stack(s_ffn, axis=0))
```

```python
import functools
import math

import numpy as np
import jax
import jax.numpy as jnp
from jax import lax
from jax.experimental import pallas as pl
from jax.experimental.pallas import tpu as pltpu

D_MODEL = 1024
D_FF = 2816
N_HEADS = 8
HEAD_DIM = 64
ATTN_WIDTH = N_HEADS * HEAD_DIM
GROUPS = ((128, 1), (512, 4), (2048, 16))
N_GROUPS = len(GROUPS)
N_KEYS = 128
N_BUCKETS = 32
MAX_DISTANCE = 2048
EPS = 1e-6
NEG = -1e30
SCALE = HEAD_DIM ** -0.5
LOG2E = math.log2(math.e)
SUPER = 2048
QBLK = 128
LANES = 128
ATTN_UNROLL = 8
BF = jnp.bfloat16
F32 = jnp.float32

V7X_VMEM_BYTES = 64 * 1024 * 1024
VMEM_LIMIT = V7X_VMEM_BYTES - 8 * 1024 * 1024


def _cparams(sem):
    return pltpu.CompilerParams(dimension_semantics=sem, vmem_limit_bytes=VMEM_LIMIT)


def _const_spec(shape):
    nd = len(shape)
    return pl.BlockSpec(shape, lambda *_: (0,) * nd, pipeline_mode=pl.Buffered(1))


def _whole_spec(shape):
    nd = len(shape)
    return pl.BlockSpec(shape, lambda *_: (0,) * nd)


def _layer_spec(arr, layer):
    nd = arr.ndim
    return pl.BlockSpec((None,) + arr.shape[1:], lambda *_: (layer,) + (0,) * (nd - 1),
                        pipeline_mode=pl.Buffered(1))


def _dot(a, b):
    return jnp.dot(a, b, preferred_element_type=F32)


def _rms(x, g):
    r = lax.rsqrt(jnp.mean(x * x, axis=-1, keepdims=True) + EPS)
    return (x * r) * g


def _silu(z):
    half = 0.5 * z
    return half + half * jnp.tanh(half)


def _head_rms(x, gain, bd):
    ss = _dot((x * x).astype(BF), bd)
    r = lax.rsqrt(ss * (1.0 / HEAD_DIM) + EPS)
    return (x * r) * gain


def _conv3(buf, cw_ref, tm):
    ext = buf[0:tm + 8, :]
    back2 = pltpu.roll(ext, 2, axis=0)[8:8 + tm, :]
    back1 = pltpu.roll(ext, 1, axis=0)[8:8 + tm, :]
    out = cw_ref[0:1, :] * back2
    out = out + cw_ref[1:2, :] * back1
    return out + cw_ref[2:3, :] * ext[8:8 + tm, :]


def _mixer_body(x_ref, g_ref, win_ref, cw_ref, wout_ref, o_ref, hist_ref, ubuf, *, tm, nt):
    t = pl.program_id(1)

    @pl.when(t == 0)
    def _():
        ubuf[0:8, :] = jnp.zeros((8, D_MODEL), F32)

    x = x_ref[0]
    h = _rms(x, g_ref[...]).astype(BF)
    bg = _dot(h, win_ref[:, 0:D_MODEL])
    cg = _dot(h, win_ref[:, D_MODEL:2 * D_MODEL])
    xv = _dot(h, win_ref[:, 2 * D_MODEL:3 * D_MODEL])
    ubuf[8:8 + tm, :] = cg * xv
    conv = _conv3(ubuf, cw_ref, tm)
    y = _dot((bg * conv).astype(BF), wout_ref[...])
    o_ref[0] = x + y
    tail = ubuf[tm:tm + 8, :]
    ubuf[0:8, :] = tail

    @pl.when(t == nt - 1)
    def _():
        hist_ref[0] = tail[6:8, :]


def _mixer_prompt(x, layer, mix_layer, g, w_in, conv_w, w_out, tm=512):
    b, t, d = x.shape
    nt = t // tm
    return pl.pallas_call(
        functools.partial(_mixer_body, tm=tm, nt=nt),
        grid=(b, nt),
        in_specs=[
            pl.BlockSpec((1, tm, d), lambda i, j: (i, j, 0)),
            _layer_spec(g, layer),
            *[_layer_spec(a, mix_layer) for a in (w_in, conv_w, w_out)],
        ],
        out_specs=[
            pl.BlockSpec((1, tm, d), lambda i, j: (i, j, 0)),
            pl.BlockSpec((1, 2, d), lambda i, j: (i, 0, 0)),
        ],
        out_shape=[
            jax.ShapeDtypeStruct((b, t, d), F32),
            jax.ShapeDtypeStruct((b, 2, d), F32),
        ],
        scratch_shapes=[pltpu.VMEM((tm + 8, d), F32)],
        compiler_params=_cparams(("arbitrary", "arbitrary")),
        name="mixer_prompt",
    )(x, g, w_in, conv_w, w_out)


def _sample_heads(first_head, q_ref, kvn_ref, c_refs, bt_refs, b0_ref, o_ref, n_refs):
    heads = o_ref.shape[-1]
    head_lane = lax.broadcasted_iota(jnp.int32, (HEAD_DIM, N_HEADS), 1)
    out_lane = lax.broadcasted_iota(jnp.int32, (HEAD_DIM, heads), 1)
    out = jnp.zeros((HEAD_DIM, heads), F32)
    for hh in range(heads):
        h = first_head + hh
        mine = head_lane == h
        column = lambda x: jnp.sum(jnp.where(mine, x, 0.0), axis=1, keepdims=True)
        ms, ss, nums = [], [], []
        for g in range(N_GROUPS):
            c_ref, n_ref = c_refs[g], n_refs[g]
            keep = c_ref.shape[-1]
            qc = column(q_ref[0, g])
            kn = column(kvn_ref[0, g, 0])
            vn = column(kvn_ref[0, g, 1])
            kt = c_ref[0, 0, 0, hh]
            vt = c_ref[0, 0, 1, hh]
            l = jnp.sum(kt * qc, axis=0, keepdims=True) + bt_refs[g][h]
            ln = jnp.sum(kn * qc, axis=0, keepdims=True) + b0_ref[g, h]
            m = jnp.maximum(jnp.max(l, axis=1, keepdims=True), ln)
            e = jnp.exp2(l - m)
            en = jnp.exp2(ln - m)
            ss.append(jnp.sum(e, axis=1, keepdims=True) + en)
            nums.append(jnp.sum(vt * e, axis=1, keepdims=True) + vn * en)
            ms.append(m)
            n_ref[0, 0, 0, hh] = pltpu.roll(kt, keep - 1, axis=1)
            n_ref[0, 0, 1, hh] = pltpu.roll(vt, keep - 1, axis=1)
            n_ref[0, 0, 0, hh, :, keep - 1:keep] = kn
            n_ref[0, 0, 1, hh, :, keep - 1:keep] = vn
        mm = jnp.maximum(jnp.maximum(ms[0], ms[1]), ms[2])
        w = [jnp.exp2(m - mm) for m in ms]
        den = w[0] * ss[0] + w[1] * ss[1] + w[2] * ss[2]
        num = w[0] * nums[0] + w[1] * nums[1] + w[2] * nums[2]
        out = jnp.where(out_lane == hh, num / den, out)
    o_ref[0] = out


N_SAMPLE_IN = 9


def _ffn_body(*refs, tm, nt, has_pre, n_sample_in, n_alias):
    if has_pre:
        x_ref, a_ref, wa_ref = refs[:3]
        refs = refs[3:]
    else:
        x_ref = refs[0]
        refs = refs[1:]
    g_ref, wg_ref, wu_ref, cw_ref, cb_ref, wd_ref = refs[:6]
    sample_in = refs[6:6 + n_sample_in]
    outs = refs[6 + n_sample_in + n_alias:-1]
    o_ref, hist_ref, sample_out = outs[0], outs[1], outs[2:]
    gbuf = refs[-1]
    t = pl.program_id(1)

    @pl.when(t == 0)
    def _():
        gbuf[0:8, :] = jnp.zeros((8, D_FF), F32)

    if n_sample_in:
        q_ref, kvn_ref = sample_in[0:2]
        heads = sample_out[0].shape[-1]
        step = pl.program_id(0) * nt + t
        _sample_heads((step % (N_HEADS // heads)) * heads, q_ref, kvn_ref, sample_in[2:5], sample_in[5:8],
                      sample_in[8], sample_out[0], sample_out[1:4])
    x = x_ref[0]
    if has_pre:
        x = x + _dot(a_ref[0].astype(BF), wa_ref[...])
    h = _rms(x, g_ref[...]).astype(BF)
    gbuf[8:8 + tm, :] = _dot(h, wg_ref[...])
    up = _dot(h, wu_ref[...])
    z = _conv3(gbuf, cw_ref, tm) + cb_ref[...]
    act = _silu(z) * up
    o_ref[0] = x + _dot(act.astype(BF), wd_ref[...])
    tail = gbuf[tm:tm + 8, :]
    gbuf[0:8, :] = tail

    @pl.when(t == nt - 1)
    def _():
        hist_ref[0] = tail[6:8, :]


def _ffn_prompt(x, pre, layer, g, w_gate, w_up, conv_w, conv_b, w_down, sample=None):
    b, t, d = x.shape
    tm = 512 if sample is None else 256
    nt = t // tm
    row = lambda i, j: (i, j, 0)
    in_specs = [pl.BlockSpec((1, tm, d), row)]
    args = [x]
    if pre is not None:
        a, wa, pre_layer = pre
        in_specs += [pl.BlockSpec((1, tm, a.shape[-1]), row), _layer_spec(wa, pre_layer)]
        args += [a, wa]
    weights = (g, w_gate, w_up, conv_w, conv_b, w_down)
    in_specs += [_layer_spec(a, layer) for a in weights]
    args += weights
    out_specs = [
        pl.BlockSpec((1, tm, d), row),
        pl.BlockSpec((1, 2, D_FF), lambda i, j: (i, 0, 0)),
    ]
    out_shape = [
        jax.ShapeDtypeStruct((b, t, d), F32),
        jax.ShapeDtypeStruct((b, 2, D_FF), F32),
    ]
    aliases = {}
    if sample is not None:
        q, kvn, caches_t, attn_layer, bts, b0, prev = sample
        nb = q.shape[0]
        steps = b * nt
        parts = steps // nb
        heads = N_HEADS // parts
        assert parts * nb == steps and heads * parts == N_HEADS
        cache_specs = []
        for c, (win, dil) in zip(caches_t, GROUPS):
            assert c.shape[-1] == win, "the cache must hold one full window"
            cache_specs.append(pl.BlockSpec(
                (1, 1, 2, heads, HEAD_DIM, win),
                lambda i, j: (attn_layer, (i * nt + j) // parts, 0, (i * nt + j) % parts, 0, 0)))
        in_specs += [
            pl.BlockSpec((1,) + q.shape[1:], lambda i, j: ((i * nt + j) // parts, 0, 0, 0)),
            pl.BlockSpec((1,) + kvn.shape[1:], lambda i, j: ((i * nt + j) // parts, 0, 0, 0, 0)),
            *cache_specs,
            *[_const_spec(bt.shape) for bt in bts],
            _const_spec(b0.shape),
        ]
        args += [q, kvn, *caches_t, *bts, b0]
        if prev is not None:
            in_specs += [pl.BlockSpec(memory_space=pl.ANY)] * N_GROUPS
            aliases = {len(args) + k: 3 + k for k in range(N_GROUPS)}
            args += list(prev)
        out_specs += [pl.BlockSpec((1, HEAD_DIM, heads), lambda i, j: (i * nt + j, 0, 0)), *cache_specs]
        out_shape += [jax.ShapeDtypeStruct((steps, HEAD_DIM, heads), F32)]
        out_shape += [jax.ShapeDtypeStruct(c.shape, c.dtype) for c in caches_t]
    return pl.pallas_call(
        functools.partial(_ffn_body, tm=tm, nt=nt, has_pre=pre is not None,
                          n_sample_in=0 if sample is None else N_SAMPLE_IN, n_alias=len(aliases)),
        grid=(b, nt),
        in_specs=in_specs,
        out_specs=out_specs,
        out_shape=out_shape,
        input_output_aliases=aliases,
        scratch_shapes=[pltpu.VMEM((tm + 8, D_FF), F32)],
        compiler_params=_cparams(("arbitrary", "arbitrary")),
        name="ffn_prompt",
    )(*args)


def _qkv_body(x_ref, g_ref, w_ref, qg_ref, kg_ref, bd_ref, *refs, tm, nt):
    qkv_refs = refs[:3 * N_GROUPS]
    cache_refs = refs[3 * N_GROUPS:4 * N_GROUPS]
    t = pl.program_id(1)
    h = _rms(x_ref[0], g_ref[...]).astype(BF)
    bd = bd_ref[...]
    for g, (win, dil) in reversed(list(enumerate(GROUPS))):
        c0 = g * 3 * ATTN_WIDTH
        q = _dot(h, w_ref[:, c0:c0 + ATTN_WIDTH])
        k = _dot(h, w_ref[:, c0 + ATTN_WIDTH:c0 + 2 * ATTN_WIDTH])
        v = _dot(h, w_ref[:, c0 + 2 * ATTN_WIDTH:c0 + 3 * ATTN_WIDTH])
        q = _head_rms(q, qg_ref[...], bd)
        k = _head_rms(k, kg_ref[...], bd)

        c_ref = cache_refs[g]
        keep = win
        if keep >= tm:
            first = nt - keep // tm

            @pl.when(t >= first)
            def _():
                c_ref[0, :, 0:ATTN_WIDTH] = k
                c_ref[0, :, ATTN_WIDTH:2 * ATTN_WIDTH] = v
        else:
            @pl.when(t == nt - 1)
            def _():
                c_ref[0, :, 0:ATTN_WIDTH] = k[tm - keep:, :]
                c_ref[0, :, ATTN_WIDTH:2 * ATTN_WIDTH] = v[tm - keep:, :]

        for o_ref, val in zip(qkv_refs[3 * g:3 * g + 3], (q, k, v)):
            if dil == 1:
                o_ref[0, 0, 0] = val.astype(BF)
            else:
                by_residue = pltpu.einshape("abc->bac", val.reshape(tm // dil, dil, ATTN_WIDTH))
                o_ref[0, 0] = by_residue.astype(BF)


def _qkv_prompt(x, layer, attn_layer, g, w_qkv, q_gain, k_gain, bd, tm=512):
    b, t, d = x.shape
    nt = t // tm
    tiles_per_super = SUPER // tm
    out_specs, out_shapes = [], []
    for win, dil in GROUPS:
        rows = tm // dil
        shp = (b, t // SUPER, dil, SUPER // dil, ATTN_WIDTH)
        spec = pl.BlockSpec(
            (1, 1, dil, rows, ATTN_WIDTH),
            lambda i, j: (i, j // tiles_per_super, 0, j % tiles_per_super, 0))
        out_specs += [spec] * 3
        out_shapes += [jax.ShapeDtypeStruct(shp, BF)] * 3
    for win, dil in GROUPS:
        keep = win
        if keep >= tm:
            first = nt - keep // tm
            spec = pl.BlockSpec((1, tm, 2 * ATTN_WIDTH),
                                lambda i, j, first=first: (i, jnp.maximum(j - first, 0), 0))
        else:
            spec = pl.BlockSpec((1, keep, 2 * ATTN_WIDTH), lambda i, j: (i, 0, 0))
        out_specs.append(spec)
        out_shapes.append(jax.ShapeDtypeStruct((b, keep, 2 * ATTN_WIDTH), F32))
    res = pl.pallas_call(
        functools.partial(_qkv_body, tm=tm, nt=nt),
        grid=(b, nt),
        in_specs=[
            pl.BlockSpec((1, tm, d), lambda i, j: (i, j, 0)),
            _layer_spec(g, layer),
            *[_layer_spec(a, attn_layer) for a in (w_qkv, q_gain, k_gain)],
            _const_spec((ATTN_WIDTH, ATTN_WIDTH)),
        ],
        out_specs=out_specs,
        out_shape=out_shapes,
        compiler_params=_cparams(("arbitrary", "arbitrary")),
        name="qkv_prompt",
    )(x, g, w_qkv, q_gain, k_gain, bd)
    qkv = [r.reshape(b, t, ATTN_WIDTH) for r in res[:3 * N_GROUPS]]
    return qkv, res[3 * N_GROUPS:]


def _attn_body(*refs):
    qkv_refs = refs[:3 * N_GROUPS]
    tb_ref = refs[3 * N_GROUPS]
    o_ref = refs[3 * N_GROUPS + 1]
    out_s, lse_s = refs[3 * N_GROUPS + 2:]
    n_super = o_ref.shape[1] // SUPER
    blocks = SUPER // QBLK
    lane = lax.broadcasted_iota(jnp.int32, (QBLK, LANES), 1)
    first_head = lane < HEAD_DIM
    contract_last = (((1,), (1,)), ((), ()))
    zero = jnp.zeros((), BF)
    lane2 = lax.broadcasted_iota(jnp.int32, (2 * QBLK, LANES), 1)
    first_head2 = lane2 < HEAD_DIM
    ones_h0 = jnp.where(first_head2, 1.0, 0.0).astype(BF)
    ones_h1 = jnp.where(first_head2, 0.0, 1.0).astype(BF)

    for s in range(n_super):
        for g, (win, dil) in enumerate(GROUPS):
            q_ref, k_ref, v_ref = qkv_refs[3 * g:3 * g + 3]
            per_res = blocks // dil

            def trip(it, carry, s=s, g=g, dil=dil, per_res=per_res,
                     q_ref=q_ref, k_ref=k_ref, v_ref=v_ref):
                def logits_of(u):
                    n = it * ATTN_UNROLL + u
                    c = n % per_res
                    r = n // per_res
                    start = s * SUPER + n * QBLK
                    if s == 0:
                        prev = jnp.where(c > 0, start - QBLK, start)
                        first = jnp.where(c == 0, 1, 0)
                    else:
                        prev = jnp.where(c > 0, start - QBLK, start - SUPER + SUPER // dil - QBLK)
                        first = 0
                    start = pl.multiple_of(start, QBLK)
                    prev = pl.multiple_of(prev, QBLK)
                    qb = q_ref[0, pl.ds(start, QBLK), :]
                    q2 = jnp.concatenate(
                        [jnp.where(first_head, qb, zero), jnp.where(first_head, zero, qb)], axis=0)
                    k2 = jnp.concatenate(
                        [k_ref[0, pl.ds(prev, QBLK), :], k_ref[0, pl.ds(start, QBLK), :]], axis=0)
                    logits = lax.dot_general(q2, k2, contract_last, preferred_element_type=F32)
                    return logits + tb_ref[g, first, 0], start, prev, r + dil * QBLK * c

                def probs_of(logits, start, prev, nat):
                    m = jnp.max(logits, axis=1, keepdims=True)
                    return jnp.exp2(logits - m).astype(BF), m, start, prev, nat

                def finish(e, m, start, prev, nat):
                    v2 = jnp.concatenate(
                        [v_ref[0, pl.ds(prev, QBLK), :], v_ref[0, pl.ds(start, QBLK), :]], axis=0)
                    rhs = jnp.concatenate([
                        jnp.concatenate([jnp.where(first_head2, v2, zero), ones_h0], axis=1),
                        jnp.concatenate([jnp.where(first_head2, zero, v2), ones_h1], axis=1)], axis=0)
                    pv = _dot(jnp.concatenate([e[0:QBLK], e[QBLK:]], axis=1), rhs)
                    if dil == 1:
                        rows = pl.ds(pl.multiple_of(nat, QBLK), QBLK)
                    else:
                        rows = pl.ds(nat, QBLK, stride=dil)
                    row_sum = pv[:, LANES:]
                    out_s[g, rows, :] = pv[:, 0:LANES] / row_sum
                    lse_s[g, rows, :] = jnp.where(first_head, m[0:QBLK], m[QBLK:]) + jnp.log2(row_sum)

                staged, probs = {}, {}
                for u in range(ATTN_UNROLL + 2):
                    if u < ATTN_UNROLL:
                        staged[u] = logits_of(u)
                    if 1 <= u <= ATTN_UNROLL:
                        probs[u - 1] = probs_of(*staged.pop(u - 1))
                    if u >= 2:
                        finish(*probs.pop(u - 2))
                return carry

            lax.fori_loop(0, blocks // ATTN_UNROLL, trip, 0)

        mrows = 256

        def merge(i, carry, s=s):
            rows = pl.ds(pl.multiple_of(i * mrows, mrows), mrows)
            l0, l1, l2 = lse_s[0, rows, :], lse_s[1, rows, :], lse_s[2, rows, :]
            top = jnp.maximum(jnp.maximum(l0, l1), l2)
            w0, w1, w2 = jnp.exp2(l0 - top), jnp.exp2(l1 - top), jnp.exp2(l2 - top)
            num = w0 * out_s[0, rows, :] + w1 * out_s[1, rows, :] + w2 * out_s[2, rows, :]
            o_ref[0, pl.ds(pl.multiple_of(s * SUPER + i * mrows, mrows), mrows), :] = num / (w0 + w1 + w2)
            return carry

        lax.fori_loop(0, SUPER // mrows, merge, 0)


def _attn_prompt(qkv, tables):
    b, t, _ = qkv[0].shape
    pairs = N_HEADS // 2
    col = lambda i, j: (i, 0, j)
    return pl.pallas_call(
        _attn_body,
        grid=(b, pairs),
        in_specs=[pl.BlockSpec((1, t, LANES), col)] * (3 * N_GROUPS) + [
            pl.BlockSpec((N_GROUPS, 2, 1, 2 * QBLK, 2 * QBLK), lambda i, j: (0, 0, j, 0, 0)),
        ],
        out_specs=pl.BlockSpec((1, t, LANES), col),
        out_shape=jax.ShapeDtypeStruct((b, t, ATTN_WIDTH), F32),
        scratch_shapes=[pltpu.VMEM((N_GROUPS, SUPER, LANES), F32)] * 2,
        compiler_params=_cparams(("arbitrary", "arbitrary")),
        name="attn_prompt",
    )(*qkv, tables)


def _mixer_sample_body(x_ref, g_ref, win_ref, cw_ref, wout_ref, h0_ref, h1_ref, o_ref, u_ref):
    x = x_ref[...]
    h = _rms(x, g_ref[...]).astype(BF)
    bg = _dot(h, win_ref[:, 0:D_MODEL])
    cg = _dot(h, win_ref[:, D_MODEL:2 * D_MODEL])
    xv = _dot(h, win_ref[:, 2 * D_MODEL:3 * D_MODEL])
    u = cg * xv
    conv = cw_ref[0:1, :] * h0_ref[...]
    conv = conv + cw_ref[1:2, :] * h1_ref[...]
    conv = conv + cw_ref[2:3, :] * u
    o_ref[...] = x + _dot((bg * conv).astype(BF), wout_ref[...])
    u_ref[...] = u


def _mixer_sample(x, layer, mix_layer, g, w_in, conv_w, w_out, h0, h1):
    n, d = x.shape
    return pl.pallas_call(
        _mixer_sample_body,
        grid=(1,),
        in_specs=[_const_spec(x.shape), _layer_spec(g, layer),
                  *[_layer_spec(a, mix_layer) for a in (w_in, conv_w, w_out)],
                  _const_spec(h0.shape), _const_spec(h1.shape)],
        out_specs=[_whole_spec((n, d))] * 2,
        out_shape=[jax.ShapeDtypeStruct((n, d), F32)] * 2,
        compiler_params=_cparams(("arbitrary",)),
        name="mixer_sample",
    )(x, g, w_in, conv_w, w_out, h0, h1)


def _ffn_sample_body(*refs, has_pre):
    if has_pre:
        x_ref, a_ref, wa_ref = refs[:3]
        refs = refs[3:]
    else:
        x_ref = refs[0]
        refs = refs[1:]
    g_ref, wg_ref, wu_ref, cw_ref, cb_ref, wd_ref, h0_ref, h1_ref, o_ref, gate_ref = refs
    x = x_ref[...]
    if has_pre:
        x = x + _dot(a_ref[...].astype(BF), wa_ref[...])
    h = _rms(x, g_ref[...]).astype(BF)
    gate = _dot(h, wg_ref[...])
    up = _dot(h, wu_ref[...])
    z = cw_ref[0:1, :] * h0_ref[...]
    z = z + cw_ref[1:2, :] * h1_ref[...]
    z = z + cw_ref[2:3, :] * gate
    z = z + cb_ref[...]
    act = _silu(z) * up
    o_ref[...] = x + _dot(act.astype(BF), wd_ref[...])
    gate_ref[...] = gate


def _ffn_sample(x, pre, layer, g, w_gate, w_up, conv_w, conv_b, w_down, h0, h1):
    n, d = x.shape
    args, in_specs = [x], [_const_spec(x.shape)]
    if pre is not None:
        a, wa, pre_layer = pre
        args += [a, wa]
        in_specs += [_const_spec(a.shape), _layer_spec(wa, pre_layer)]
    weights = (g, w_gate, w_up, conv_w, conv_b, w_down)
    args += [*weights, h0, h1]
    in_specs += [*[_layer_spec(a, layer) for a in weights], _const_spec(h0.shape), _const_spec(h1.shape)]
    return pl.pallas_call(
        functools.partial(_ffn_sample_body, has_pre=pre is not None),
        grid=(1,),
        in_specs=in_specs,
        out_specs=[_whole_spec((n, d)), _whole_spec((n, D_FF))],
        out_shape=[jax.ShapeDtypeStruct((n, d), F32), jax.ShapeDtypeStruct((n, D_FF), F32)],
        compiler_params=_cparams(("arbitrary",)),
        name="ffn_sample",
    )(*args)


def _qkv_sample_body(x_ref, g_ref, w_ref, qg_ref, kg_ref, bd_ref, q_ref, kv_ref):
    h = _rms(x_ref[...], g_ref[...]).astype(BF)
    bd = bd_ref[...]
    for g in range(N_GROUPS):
        c0 = g * 3 * ATTN_WIDTH
        q = _dot(h, w_ref[:, c0:c0 + ATTN_WIDTH])
        k = _dot(h, w_ref[:, c0 + ATTN_WIDTH:c0 + 2 * ATTN_WIDTH])
        v = _dot(h, w_ref[:, c0 + 2 * ATTN_WIDTH:c0 + 3 * ATTN_WIDTH])
        q_ref[:, g * ATTN_WIDTH:(g + 1) * ATTN_WIDTH] = _head_rms(q, qg_ref[...], bd)
        kv_ref[:, 2 * g * ATTN_WIDTH:(2 * g + 1) * ATTN_WIDTH] = _head_rms(k, kg_ref[...], bd)
        kv_ref[:, (2 * g + 1) * ATTN_WIDTH:(2 * g + 2) * ATTN_WIDTH] = v


def _qkv_sample(x, layer, attn_layer, g, w_qkv, q_gain, k_gain, bd):
    n = x.shape[0]
    out_cols = (N_GROUPS * ATTN_WIDTH, N_GROUPS * 2 * ATTN_WIDTH)
    return pl.pallas_call(
        _qkv_sample_body,
        grid=(1,),
        in_specs=[_const_spec(x.shape), _layer_spec(g, layer),
                  *[_layer_spec(a, attn_layer) for a in (w_qkv, q_gain, k_gain)], _const_spec(bd.shape)],
        out_specs=[_whole_spec((n, c)) for c in out_cols],
        out_shape=[jax.ShapeDtypeStruct((n, c), F32) for c in out_cols],
        compiler_params=_cparams(("arbitrary",)),
        name="qkv_sample",
    )(x, g, w_qkv, q_gain, k_gain, bd)


def _t5_bucket(dist):
    exact = N_BUCKETS // 2
    n = np.asarray(dist, dtype=np.float32)
    large = exact + np.log(np.maximum(n, 1.0) / exact) / math.log(MAX_DISTANCE / exact) * (N_BUCKETS - exact)
    large = np.minimum(np.floor(large), N_BUCKETS - 1)
    return np.where(n < exact, n, large).astype(np.int32)


def _bias_tables(rel_bias):
    ci = np.arange(2 * QBLK)[None, None, :]
    period = 3 * QBLK
    tabs, bts, b0s = [], [], []
    for g, (win, dil) in enumerate(GROUPS):
        bucket = _t5_bucket(dil * np.arange(N_KEYS + 1))
        bias = jnp.take(rel_bias, bucket, axis=0)[:, g * N_HEADS:(g + 1) * N_HEADS].T
        bias = bias * LOG2E
        w = jnp.concatenate([bias[:, ::-1], jnp.full((N_HEADS, period - N_KEYS - 1), NEG, F32)], axis=1)
        band = jnp.tile(w, (1, QBLK))[:, :QBLK * (period - 1)]
        band = band.reshape(N_HEADS, QBLK, period - 1)[:, :, :2 * QBLK]
        band_first = jnp.where(ci >= QBLK, band, NEG)
        pair_rows = (N_HEADS // 2, 2 * QBLK, 2 * QBLK)
        tabs.append(jnp.stack([band.reshape(pair_rows), band_first.reshape(pair_rows)], axis=0))
        vals = bias[:, N_KEYS:0:-1][:, :, None]
        skipped = jnp.full((N_HEADS, N_KEYS, dil - 1), NEG, F32)
        bts.append(jnp.concatenate([vals, skipped], axis=2).reshape(N_HEADS, 1, win))
        b0s.append(bias[:, 0:1][:, :, None])
    return jnp.stack(tabs), bts, jnp.stack(b0s)


def kernel(x_prompt, x_sample, state_sc_conv, cache_kv_d1, cache_kv_d4, cache_kv_d16, state_ffn_conv,
           norm_mix, norm_ffn, sc_w_in, sc_conv_w, sc_w_out, attn_w_qkv, attn_q_norm, attn_k_norm,
           attn_w_out, rel_bias, ffn_w_gate, ffn_w_up, ffn_conv_w, ffn_conv_b, ffn_w_down):
    depth = norm_mix.shape[0]
    nb = x_sample.shape[0]
    caches_t = [jnp.transpose(c, (0, 1, 3, 4, 5, 2)) for c in (cache_kv_d1, cache_kv_d4, cache_kv_d16)]
    tables, bts, b0 = _bias_tables(rel_bias)
    seg = np.arange(ATTN_WIDTH) // HEAD_DIM
    bd = jnp.asarray(seg[:, None] == seg[None, :], BF)

    mixer_w = (sc_w_in.astype(BF), sc_conv_w, sc_w_out.astype(BF))
    w_qkv_b, w_ao_b = attn_w_qkv.astype(BF), attn_w_out.astype(BF)
    ffn_w = (norm_ffn[:, None, :], ffn_w_gate.astype(BF), ffn_w_up.astype(BF), ffn_conv_w,
             ffn_conv_b[:, None, :], ffn_w_down.astype(BF))
    g_mix = norm_mix[:, None, :]
    q_gain = (jnp.tile(attn_q_norm, (1, N_HEADS)) * (SCALE * LOG2E))[:, None, :]
    k_gain = jnp.tile(attn_k_norm, (1, N_HEADS))[:, None, :]

    xp = x_prompt
    xs = x_sample[:, 0, :]
    p_sc, s_sc, p_ffn, s_ffn = [], [], [], []
    p_kv = [[] for _ in range(N_GROUPS)]
    new_caches = None
    for i in range(depth):
        j = i // 2
        sf = state_ffn_conv[i]
        if i % 2 == 0:
            xp, hist = _mixer_prompt(xp, i, j, g_mix, *mixer_w)
            p_sc.append(hist)
            st = state_sc_conv[j]
            xs, u = _mixer_sample(xs, i, j, g_mix, *mixer_w, st[:, 0], st[:, 1])
            s_sc.append(jnp.stack([st[:, 1], u], axis=1))
            pre_p = pre_s = None
        else:
            qkv, kv_rows = _qkv_prompt(xp, i, j, g_mix, w_qkv_b, q_gain, k_gain, bd)
            for g in range(N_GROUPS):
                p_kv[g].append(kv_rows[g].reshape(kv_rows[g].shape[:2] + (2, N_HEADS, HEAD_DIM)))
            pre_p = (_attn_prompt(qkv, tables), w_ao_b, j)

            qs, kvs = _qkv_sample(xs, i, j, g_mix, w_qkv_b, q_gain, k_gain, bd)
            qs = jnp.swapaxes(qs.reshape(nb, N_GROUPS, N_HEADS, HEAD_DIM), -1, -2)
            kvs = jnp.swapaxes(kvs.reshape(nb, N_GROUPS, 2, N_HEADS, HEAD_DIM), -1, -2)
            xp, fh, a_s, *new_caches = _ffn_prompt(
                xp, pre_p, i, *ffn_w, sample=(qs, kvs, caches_t, j, bts, b0, new_caches))
            a_s = jnp.swapaxes(a_s.reshape(nb, -1, HEAD_DIM, a_s.shape[-1]), -1, -2)
            pre_s = (a_s.reshape(nb, ATTN_WIDTH), w_ao_b, j)
        if i % 2 == 0:
            xp, fh = _ffn_prompt(xp, pre_p, i, *ffn_w)
        p_ffn.append(fh)
        xs, gate = _ffn_sample(xs, pre_s, i, *ffn_w, sf[:, 0], sf[:, 1])
        s_ffn.append(jnp.stack([sf[:, 1], gate], axis=1))

    s_kv = [jnp.transpose(c, (0, 1, 5, 2, 3, 4)) for c in new_caches]
    return (xp, xs[:, None, :],
            jnp.stack(p_sc, axis=0), jnp.stack(p_kv[0], axis=0), jnp.stack(p_kv[1], axis=0),
            jnp.stack(p_kv[2], axis=0), jnp.stack(p_ffn, axis=0),
            jnp.stack(s_sc, axis=0), s_kv[0], s_kv[1], s_kv[2], jnp.stack(s_ffn, axis=0))
```

```python
import functools
import math

import numpy as np
import jax
import jax.numpy as jnp
from jax import lax
from jax.experimental import pallas as pl
from jax.experimental.pallas import tpu as pltpu

D_MODEL = 1024
D_FF = 2816
N_HEADS = 8
HEAD_DIM = 64
ATTN_WIDTH = N_HEADS * HEAD_DIM
GROUPS = ((128, 1), (512, 4), (2048, 16))
N_GROUPS = len(GROUPS)
N_KEYS = 128
N_BUCKETS = 32
MAX_DISTANCE = 2048
EPS = 1e-6
NEG = -1e30
SCALE = HEAD_DIM ** -0.5
LOG2E = math.log2(math.e)
SUPER = 2048
QBLK = 128
LANES = 128
ATTN_UNROLL = 16
BF = jnp.bfloat16
F32 = jnp.float32

V7X_VMEM_BYTES = 64 * 1024 * 1024
VMEM_LIMIT = V7X_VMEM_BYTES - 8 * 1024 * 1024


def _cparams(sem):
    return pltpu.CompilerParams(dimension_semantics=sem, vmem_limit_bytes=VMEM_LIMIT)


def _const_spec(shape):
    nd = len(shape)
    return pl.BlockSpec(shape, lambda *_: (0,) * nd, pipeline_mode=pl.Buffered(1))


def _whole_spec(shape):
    nd = len(shape)
    return pl.BlockSpec(shape, lambda *_: (0,) * nd)


def _layer_spec(arr, layer):
    nd = arr.ndim
    return pl.BlockSpec((None,) + arr.shape[1:], lambda *_: (layer,) + (0,) * (nd - 1),
                        pipeline_mode=pl.Buffered(1))


def _dot(a, b):
    return jnp.dot(a, b, preferred_element_type=F32)


def _rms(x, g):
    r = lax.rsqrt(jnp.mean(x * x, axis=-1, keepdims=True) + EPS)
    return (x * r) * g


def _silu(z):
    half = 0.5 * z
    return half + half * jnp.tanh(half)


def _head_rms(x, gain, bd):
    ss = _dot((x * x).astype(BF), bd)
    r = lax.rsqrt(ss * (1.0 / HEAD_DIM) + EPS)
    return (x * r) * gain


def _conv3(buf, cw_ref, tm):
    ext = buf[0:tm + 8, :]
    back2 = pltpu.roll(ext, 2, axis=0)[8:8 + tm, :]
    back1 = pltpu.roll(ext, 1, axis=0)[8:8 + tm, :]
    out = cw_ref[0:1, :] * back2
    out = out + cw_ref[1:2, :] * back1
    return out + cw_ref[2:3, :] * ext[8:8 + tm, :]


def _mixer_body(x_ref, g_ref, win_ref, cw_ref, wout_ref, o_ref, hist_ref, ubuf, *, tm):
    t = pl.program_id(1)

    @pl.when(t == 0)
    def _():
        ubuf[0:8, :] = jnp.zeros((8, D_MODEL), F32)

    x = x_ref[0]
    h = _rms(x, g_ref[...]).astype(BF)
    bg = _dot(h, win_ref[:, 0:D_MODEL])
    cg = _dot(h, win_ref[:, D_MODEL:2 * D_MODEL])
    xv = _dot(h, win_ref[:, 2 * D_MODEL:3 * D_MODEL])
    ubuf[8:8 + tm, :] = cg * xv
    conv = _conv3(ubuf, cw_ref, tm)
    y = _dot((bg * conv).astype(BF), wout_ref[...])
    o_ref[0] = x + y
    tail = ubuf[tm:tm + 8, :]
    ubuf[0:8, :] = tail
    hist_ref[0] = tail[6:8, :]


def _mixer_prompt(x, layer, mix_layer, g, w_in, conv_w, w_out, tm=512):
    b, t, d = x.shape
    nt = t // tm
    return pl.pallas_call(
        functools.partial(_mixer_body, tm=tm),
        grid=(b, nt),
        in_specs=[
            pl.BlockSpec((1, tm, d), lambda i, j: (i, j, 0)),
            _layer_spec(g, layer),
            *[_layer_spec(a, mix_layer) for a in (w_in, conv_w, w_out)],
        ],
        out_specs=[
            pl.BlockSpec((1, tm, d), lambda i, j: (i, j, 0)),
            pl.BlockSpec((1, 2, d), lambda i, j: (i, 0, 0)),
        ],
        out_shape=[
            jax.ShapeDtypeStruct((b, t, d), F32),
            jax.ShapeDtypeStruct((b, 2, d), F32),
        ],
        scratch_shapes=[pltpu.VMEM((tm + 8, d), F32)],
        compiler_params=_cparams(("arbitrary", "arbitrary")),
        name="mixer_prompt",
    )(x, g, w_in, conv_w, w_out)


def _sample_heads(first_head, q_ref, kvn_ref, c_refs, bt_refs, b0_ref, o_ref, n_refs):
    heads = o_ref.shape[-1]
    head_lane = lax.broadcasted_iota(jnp.int32, (HEAD_DIM, N_HEADS), 1)
    out_lane = lax.broadcasted_iota(jnp.int32, (HEAD_DIM, heads), 1)
    out = jnp.zeros((HEAD_DIM, heads), F32)
    for hh in range(heads):
        h = first_head + hh
        mine = head_lane == h
        column = lambda x: jnp.sum(jnp.where(mine, x, 0.0), axis=1, keepdims=True)
        ms, ss, nums = [], [], []
        for g in range(N_GROUPS):
            c_ref, n_ref = c_refs[g], n_refs[g]
            keep = c_ref.shape[-1]
            qc = column(q_ref[0, g])
            kn = column(kvn_ref[0, g, 0])
            vn = column(kvn_ref[0, g, 1])
            kt = c_ref[0, 0, 0, hh]
            vt = c_ref[0, 0, 1, hh]
            l = jnp.sum(kt * qc, axis=0, keepdims=True) + bt_refs[g][h]
            ln = jnp.sum(kn * qc, axis=0, keepdims=True) + b0_ref[g, h]
            m = jnp.maximum(jnp.max(l, axis=1, keepdims=True), ln)
            e = jnp.exp2(l - m)
            en = jnp.exp2(ln - m)
            ss.append(jnp.sum(e, axis=1, keepdims=True) + en)
            nums.append(jnp.sum(vt * e, axis=1, keepdims=True) + vn * en)
            ms.append(m)
            n_ref[0, 0, 0, hh] = pltpu.roll(kt, keep - 1, axis=1)
            n_ref[0, 0, 1, hh] = pltpu.roll(vt, keep - 1, axis=1)
            n_ref[0, 0, 0, hh, :, keep - 1:keep] = kn
            n_ref[0, 0, 1, hh, :, keep - 1:keep] = vn
        mm = jnp.maximum(jnp.maximum(ms[0], ms[1]), ms[2])
        w = [jnp.exp2(m - mm) for m in ms]
        den = w[0] * ss[0] + w[1] * ss[1] + w[2] * ss[2]
        num = w[0] * nums[0] + w[1] * nums[1] + w[2] * nums[2]
        out = jnp.where(out_lane == hh, num / den, out)
    o_ref[0] = out


N_SAMPLE_IN = 9


def _ffn_body(*refs, tm, nt, has_pre, n_sample_in, n_alias):
    if has_pre:
        x_ref, a_ref, wa_ref = refs[:3]
        refs = refs[3:]
    else:
        x_ref = refs[0]
        refs = refs[1:]
    g_ref, wg_ref, wu_ref, cw_ref, cb_ref, wd_ref = refs[:6]
    sample_in = refs[6:6 + n_sample_in]
    outs = refs[6 + n_sample_in + n_alias:-1]
    o_ref, hist_ref, sample_out = outs[0], outs[1], outs[2:]
    gbuf = refs[-1]
    t = pl.program_id(1)

    @pl.when(t == 0)
    def _():
        gbuf[0:8, :] = jnp.zeros((8, D_FF), F32)

    if n_sample_in:
        q_ref, kvn_ref = sample_in[0:2]
        heads = sample_out[0].shape[-1]
        step = pl.program_id(0) * nt + t
        _sample_heads((step % (N_HEADS // heads)) * heads, q_ref, kvn_ref, sample_in[2:5], sample_in[5:8],
                      sample_in[8], sample_out[0], sample_out[1:4])
    x = x_ref[0]
    if has_pre:
        x = x + _dot(a_ref[0].astype(BF), wa_ref[...])
    h = _rms(x, g_ref[...]).astype(BF)
    gbuf[8:8 + tm, :] = _dot(h, wg_ref[...])
    up = _dot(h, wu_ref[...])
    z = _conv3(gbuf, cw_ref, tm) + cb_ref[...]
    act = _silu(z) * up
    o_ref[0] = x + _dot(act.astype(BF), wd_ref[...])
    tail = gbuf[tm:tm + 8, :]
    gbuf[0:8, :] = tail
    hist_ref[0] = tail[6:8, :]


def _ffn_prompt(x, pre, layer, g, w_gate, w_up, conv_w, conv_b, w_down, sample=None):
    b, t, d = x.shape
    tm = 512 if sample is None else 256
    nt = t // tm
    row = lambda i, j: (i, j, 0)
    in_specs = [pl.BlockSpec((1, tm, d), row)]
    args = [x]
    if pre is not None:
        a, wa, pre_layer = pre
        in_specs += [pl.BlockSpec((1, tm, a.shape[-1]), row), _layer_spec(wa, pre_layer)]
        args += [a, wa]
    weights = (g, w_gate, w_up, conv_w, conv_b, w_down)
    in_specs += [_layer_spec(a, layer) for a in weights]
    args += weights
    out_specs = [
        pl.BlockSpec((1, tm, d), row),
        pl.BlockSpec((1, 2, D_FF), lambda i, j: (i, 0, 0)),
    ]
    out_shape = [
        jax.ShapeDtypeStruct((b, t, d), F32),
        jax.ShapeDtypeStruct((b, 2, D_FF), F32),
    ]
    aliases = {}
    if sample is not None:
        q, kvn, caches_t, attn_layer, bts, b0, prev = sample
        nb = q.shape[0]
        steps = b * nt
        parts = steps // nb
        heads = N_HEADS // parts
        assert parts * nb == steps and heads * parts == N_HEADS
        cache_specs = []
        for c, (win, dil) in zip(caches_t, GROUPS):
            assert c.shape[-1] == win, "the cache must hold one full window"
            cache_specs.append(pl.BlockSpec(
                (1, 1, 2, heads, HEAD_DIM, win),
                lambda i, j: (attn_layer, (i * nt + j) // parts, 0, (i * nt + j) % parts, 0, 0)))
        in_specs += [
            pl.BlockSpec((1,) + q.shape[1:], lambda i, j: ((i * nt + j) // parts, 0, 0, 0)),
            pl.BlockSpec((1,) + kvn.shape[1:], lambda i, j: ((i * nt + j) // parts, 0, 0, 0, 0)),
            *cache_specs,
            *[_const_spec(bt.shape) for bt in bts],
            _const_spec(b0.shape),
        ]
        args += [q, kvn, *caches_t, *bts, b0]
        if prev is not None:
            in_specs += [pl.BlockSpec(memory_space=pl.ANY)] * N_GROUPS
            aliases = {len(args) + k: 3 + k for k in range(N_GROUPS)}
            args += list(prev)
        out_specs += [pl.BlockSpec((1, HEAD_DIM, heads), lambda i, j: (i * nt + j, 0, 0)), *cache_specs]
        out_shape += [jax.ShapeDtypeStruct((steps, HEAD_DIM, heads), F32)]
        out_shape += [jax.ShapeDtypeStruct(c.shape, c.dtype) for c in caches_t]
    return pl.pallas_call(
        functools.partial(_ffn_body, tm=tm, nt=nt, has_pre=pre is not None,
                          n_sample_in=0 if sample is None else N_SAMPLE_IN, n_alias=len(aliases)),
        grid=(b, nt),
        in_specs=in_specs,
        out_specs=out_specs,
        out_shape=out_shape,
        input_output_aliases=aliases,
        scratch_shapes=[pltpu.VMEM((tm + 8, D_FF), F32)],
        compiler_params=_cparams(("arbitrary", "arbitrary")),
        name="ffn_prompt",
    )(*args)


def _qkv_body(x_ref, g_ref, w_ref, qg_ref, kg_ref, bd_ref, *refs, tm):
    qkv_refs = refs[:3 * N_GROUPS]
    cache_refs = refs[3 * N_GROUPS:4 * N_GROUPS]
    h = _rms(x_ref[0], g_ref[...]).astype(BF)
    bd = bd_ref[...]
    for g, (win, dil) in reversed(list(enumerate(GROUPS))):
        c0 = g * 3 * ATTN_WIDTH
        q = _dot(h, w_ref[:, c0:c0 + ATTN_WIDTH])
        k = _dot(h, w_ref[:, c0 + ATTN_WIDTH:c0 + 2 * ATTN_WIDTH])
        v = _dot(h, w_ref[:, c0 + 2 * ATTN_WIDTH:c0 + 3 * ATTN_WIDTH])
        q = _head_rms(q, qg_ref[...], bd)
        k = _head_rms(k, kg_ref[...], bd)

        c_ref = cache_refs[g]
        rows = min(win, tm)
        c_ref[0, :, 0:ATTN_WIDTH] = k[tm - rows:, :]
        c_ref[0, :, ATTN_WIDTH:2 * ATTN_WIDTH] = v[tm - rows:, :]

        for o_ref, val in zip(qkv_refs[3 * g:3 * g + 3], (q, k, v)):
            if dil == 1:
                o_ref[0, 0, 0] = val.astype(BF)
            else:
                by_residue = pltpu.einshape("abc->bac", val.reshape(tm // dil, dil, ATTN_WIDTH))
                o_ref[0, 0] = by_residue.astype(BF)


def _qkv_prompt(x, layer, attn_layer, g, w_qkv, q_gain, k_gain, bd, tm=512):
    b, t, d = x.shape
    nt = t // tm
    tiles_per_super = SUPER // tm
    out_specs, out_shapes = [], []
    for win, dil in GROUPS:
        rows = tm // dil
        shp = (b, t // SUPER, dil, SUPER // dil, ATTN_WIDTH)
        spec = pl.BlockSpec(
            (1, 1, dil, rows, ATTN_WIDTH),
            lambda i, j: (i, j // tiles_per_super, 0, j % tiles_per_super, 0))
        out_specs += [spec] * 3
        out_shapes += [jax.ShapeDtypeStruct(shp, BF)] * 3
    for win, dil in GROUPS:
        keep = win
        if keep >= tm:
            first = nt - keep // tm
            spec = pl.BlockSpec((1, tm, 2 * ATTN_WIDTH),
                                lambda i, j, first=first: (i, jnp.maximum(j - first, 0), 0))
        else:
            spec = pl.BlockSpec((1, keep, 2 * ATTN_WIDTH), lambda i, j: (i, 0, 0))
        out_specs.append(spec)
        out_shapes.append(jax.ShapeDtypeStruct((b, keep, 2 * ATTN_WIDTH), F32))
    res = pl.pallas_call(
        functools.partial(_qkv_body, tm=tm),
        grid=(b, nt),
        in_specs=[
            pl.BlockSpec((1, tm, d), lambda i, j: (i, j, 0)),
            _layer_spec(g, layer),
            *[_layer_spec(a, attn_layer) for a in (w_qkv, q_gain, k_gain)],
            _const_spec((ATTN_WIDTH, ATTN_WIDTH)),
        ],
        out_specs=out_specs,
        out_shape=out_shapes,
        compiler_params=_cparams(("arbitrary", "arbitrary")),
        name="qkv_prompt",
    )(x, g, w_qkv, q_gain, k_gain, bd)
    qkv = [r.reshape(b, t, ATTN_WIDTH) for r in res[:3 * N_GROUPS]]
    return qkv, res[3 * N_GROUPS:]


def _attn_body(*refs):
    qkv_refs = refs[:3 * N_GROUPS]
    tb_ref = refs[3 * N_GROUPS]
    o_ref = refs[3 * N_GROUPS + 1]
    out_s, lse_s = refs[3 * N_GROUPS + 2:]
    n_super = o_ref.shape[1] // SUPER
    blocks = SUPER // QBLK
    lane = lax.broadcasted_iota(jnp.int32, (QBLK, LANES), 1)
    first_head = lane < HEAD_DIM
    contract_last = (((1,), (1,)), ((), ()))
    zero = jnp.zeros((), BF)
    lane2 = lax.broadcasted_iota(jnp.int32, (2 * QBLK, LANES), 1)
    first_head2 = lane2 < HEAD_DIM
    ones_h0 = jnp.where(first_head2, 1.0, 0.0).astype(BF)
    ones_h1 = jnp.where(first_head2, 0.0, 1.0).astype(BF)

    for s in range(n_super):
        for g, (win, dil) in enumerate(GROUPS):
            q_ref, k_ref, v_ref = qkv_refs[3 * g:3 * g + 3]
            per_res = blocks // dil

            def trip(it, carry, s=s, g=g, dil=dil, per_res=per_res,
                     q_ref=q_ref, k_ref=k_ref, v_ref=v_ref):
                def logits_of(u):
                    n = it * ATTN_UNROLL + u
                    c = n % per_res
                    r = n // per_res
                    start = s * SUPER + n * QBLK
                    if s == 0:
                        prev = jnp.where(c > 0, start - QBLK, start)
                        first = jnp.where(c == 0, 1, 0)
                    else:
                        prev = jnp.where(c > 0, start - QBLK, start - SUPER + SUPER // dil - QBLK)
                        first = 0
                    start = pl.multiple_of(start, QBLK)
                    prev = pl.multiple_of(prev, QBLK)
                    qb = q_ref[0, pl.ds(start, QBLK), :]
                    q2 = jnp.concatenate(
                        [jnp.where(first_head, qb, zero), jnp.where(first_head, zero, qb)], axis=0)
                    k2 = jnp.concatenate(
                        [k_ref[0, pl.ds(prev, QBLK), :], k_ref[0, pl.ds(start, QBLK), :]], axis=0)
                    logits = lax.dot_general(q2, k2, contract_last, preferred_element_type=F32)
                    return logits + tb_ref[g, first, 0], start, prev, r + dil * QBLK * c

                def probs_of(logits, start, prev, nat):
                    m = jnp.max(logits, axis=1, keepdims=True)
                    return jnp.exp2(logits - m).astype(BF), m, start, prev, nat

                def finish(e, m, start, prev, nat):
                    v2 = jnp.concatenate(
                        [v_ref[0, pl.ds(prev, QBLK), :], v_ref[0, pl.ds(start, QBLK), :]], axis=0)
                    rhs = jnp.concatenate([
                        jnp.concatenate([jnp.where(first_head2, v2, zero), ones_h0], axis=1),
                        jnp.concatenate([jnp.where(first_head2, zero, v2), ones_h1], axis=1)], axis=0)
                    pv = _dot(jnp.concatenate([e[0:QBLK], e[QBLK:]], axis=1), rhs)
                    if dil == 1:
                        rows = pl.ds(pl.multiple_of(nat, QBLK), QBLK)
                    else:
                        rows = pl.ds(nat, QBLK, stride=dil)
                    row_sum = pv[:, LANES:]
                    out_s[g, rows, :] = pv[:, 0:LANES] / row_sum
                    lse_s[g, rows, :] = jnp.where(first_head, m[0:QBLK], m[QBLK:]) + jnp.log2(row_sum)

                staged, probs = {}, {}
                for u in range(ATTN_UNROLL + 2):
                    if u < ATTN_UNROLL:
                        staged[u] = logits_of(u)
                    if 1 <= u <= ATTN_UNROLL:
                        probs[u - 1] = probs_of(*staged.pop(u - 1))
                    if u >= 2:
                        finish(*probs.pop(u - 2))
                return carry

            lax.fori_loop(0, blocks // ATTN_UNROLL, trip, 0)

        mrows = 256

        def merge(i, carry, s=s):
            rows = pl.ds(pl.multiple_of(i * mrows, mrows), mrows)
            l0, l1, l2 = lse_s[0, rows, :], lse_s[1, rows, :], lse_s[2, rows, :]
            top = jnp.maximum(jnp.maximum(l0, l1), l2)
            w0, w1, w2 = jnp.exp2(l0 - top), jnp.exp2(l1 - top), jnp.exp2(l2 - top)
            num = w0 * out_s[0, rows, :] + w1 * out_s[1, rows, :] + w2 * out_s[2, rows, :]
            o_ref[0, pl.ds(pl.multiple_of(s * SUPER + i * mrows, mrows), mrows), :] = num / (w0 + w1 + w2)
            return carry

        lax.fori_loop(0, SUPER // mrows, merge, 0)


def _attn_prompt(qkv, tables):
    b, t, _ = qkv[0].shape
    pairs = N_HEADS // 2
    col = lambda i, j: (i, 0, j)
    return pl.pallas_call(
        _attn_body,
        grid=(b, pairs),
        in_specs=[pl.BlockSpec((1, t, LANES), col)] * (3 * N_GROUPS) + [
            pl.BlockSpec((N_GROUPS, 2, 1, 2 * QBLK, 2 * QBLK), lambda i, j: (0, 0, j, 0, 0)),
        ],
        out_specs=pl.BlockSpec((1, t, LANES), col),
        out_shape=jax.ShapeDtypeStruct((b, t, ATTN_WIDTH), F32),
        scratch_shapes=[pltpu.VMEM((N_GROUPS, SUPER, LANES), F32)] * 2,
        compiler_params=_cparams(("arbitrary", "arbitrary")),
        name="attn_prompt",
    )(*qkv, tables)


def _mixer_sample_body(x_ref, g_ref, win_ref, cw_ref, wout_ref, h0_ref, h1_ref, o_ref, u_ref):
    x = x_ref[...]
    h = _rms(x, g_ref[...]).astype(BF)
    bg = _dot(h, win_ref[:, 0:D_MODEL])
    cg = _dot(h, win_ref[:, D_MODEL:2 * D_MODEL])
    xv = _dot(h, win_ref[:, 2 * D_MODEL:3 * D_MODEL])
    u = cg * xv
    conv = cw_ref[0:1, :] * h0_ref[...]
    conv = conv + cw_ref[1:2, :] * h1_ref[...]
    conv = conv + cw_ref[2:3, :] * u
    o_ref[...] = x + _dot((bg * conv).astype(BF), wout_ref[...])
    u_ref[...] = u


def _mixer_sample(x, layer, mix_layer, g, w_in, conv_w, w_out, h0, h1):
    n, d = x.shape
    return pl.pallas_call(
        _mixer_sample_body,
        grid=(1,),
        in_specs=[_const_spec(x.shape), _layer_spec(g, layer),
                  *[_layer_spec(a, mix_layer) for a in (w_in, conv_w, w_out)],
                  _const_spec(h0.shape), _const_spec(h1.shape)],
        out_specs=[_whole_spec((n, d))] * 2,
        out_shape=[jax.ShapeDtypeStruct((n, d), F32)] * 2,
        compiler_params=_cparams(("arbitrary",)),
        name="mixer_sample",
    )(x, g, w_in, conv_w, w_out, h0, h1)


def _ffn_sample_body(*refs, has_pre):
    if has_pre:
        x_ref, a_ref, wa_ref = refs[:3]
        refs = refs[3:]
    else:
        x_ref = refs[0]
        refs = refs[1:]
    g_ref, wg_ref, wu_ref, cw_ref, cb_ref, wd_ref, h0_ref, h1_ref, o_ref, gate_ref = refs
    x = x_ref[...]
    if has_pre:
        x = x + _dot(a_ref[...].astype(BF), wa_ref[...])
    h = _rms(x, g_ref[...]).astype(BF)
    gate = _dot(h, wg_ref[...])
    up = _dot(h, wu_ref[...])
    z = cw_ref[0:1, :] * h0_ref[...]
    z = z + cw_ref[1:2, :] * h1_ref[...]
    z = z + cw_ref[2:3, :] * gate
    z = z + cb_ref[...]
    act = _silu(z) * up
    o_ref[...] = x + _dot(act.astype(BF), wd_ref[...])
    gate_ref[...] = gate


def _ffn_sample(x, pre, layer, g, w_gate, w_up, conv_w, conv_b, w_down, h0, h1):
    n, d = x.shape
    args, in_specs = [x], [_const_spec(x.shape)]
    if pre is not None:
        a, wa, pre_layer = pre
        args += [a, wa]
        in_specs += [_const_spec(a.shape), _layer_spec(wa, pre_layer)]
    weights = (g, w_gate, w_up, conv_w, conv_b, w_down)
    args += [*weights, h0, h1]
    in_specs += [*[_layer_spec(a, layer) for a in weights], _const_spec(h0.shape), _const_spec(h1.shape)]
    return pl.pallas_call(
        functools.partial(_ffn_sample_body, has_pre=pre is not None),
        grid=(1,),
        in_specs=in_specs,
        out_specs=[_whole_spec((n, d)), _whole_spec((n, D_FF))],
        out_shape=[jax.ShapeDtypeStruct((n, d), F32), jax.ShapeDtypeStruct((n, D_FF), F32)],
        compiler_params=_cparams(("arbitrary",)),
        name="ffn_sample",
    )(*args)


def _qkv_sample_body(x_ref, g_ref, w_ref, qg_ref, kg_ref, bd_ref, q_ref, kv_ref):
    h = _rms(x_ref[...], g_ref[...]).astype(BF)
    bd = bd_ref[...]
    for g in range(N_GROUPS):
        c0 = g * 3 * ATTN_WIDTH
        q = _dot(h, w_ref[:, c0:c0 + ATTN_WIDTH])
        k = _dot(h, w_ref[:, c0 + ATTN_WIDTH:c0 + 2 * ATTN_WIDTH])
        v = _dot(h, w_ref[:, c0 + 2 * ATTN_WIDTH:c0 + 3 * ATTN_WIDTH])
        q_ref[:, g * ATTN_WIDTH:(g + 1) * ATTN_WIDTH] = _head_rms(q, qg_ref[...], bd)
        kv_ref[:, 2 * g * ATTN_WIDTH:(2 * g + 1) * ATTN_WIDTH] = _head_rms(k, kg_ref[...], bd)
        kv_ref[:, (2 * g + 1) * ATTN_WIDTH:(2 * g + 2) * ATTN_WIDTH] = v


def _qkv_sample(x, layer, attn_layer, g, w_qkv, q_gain, k_gain, bd):
    n = x.shape[0]
    out_cols = (N_GROUPS * ATTN_WIDTH, N_GROUPS * 2 * ATTN_WIDTH)
    return pl.pallas_call(
        _qkv_sample_body,
        grid=(1,),
        in_specs=[_const_spec(x.shape), _layer_spec(g, layer),
                  *[_layer_spec(a, attn_layer) for a in (w_qkv, q_gain, k_gain)], _const_spec(bd.shape)],
        out_specs=[_whole_spec((n, c)) for c in out_cols],
        out_shape=[jax.ShapeDtypeStruct((n, c), F32) for c in out_cols],
        compiler_params=_cparams(("arbitrary",)),
        name="qkv_sample",
    )(x, g, w_qkv, q_gain, k_gain, bd)


def _t5_bucket(dist):
    exact = N_BUCKETS // 2
    n = np.asarray(dist, dtype=np.float32)
    large = exact + np.log(np.maximum(n, 1.0) / exact) / math.log(MAX_DISTANCE / exact) * (N_BUCKETS - exact)
    large = np.minimum(np.floor(large), N_BUCKETS - 1)
    return np.where(n < exact, n, large).astype(np.int32)


def _bias_tables(rel_bias):
    ci = np.arange(2 * QBLK)[None, None, :]
    period = 3 * QBLK
    tabs, bts, b0s = [], [], []
    for g, (win, dil) in enumerate(GROUPS):
        bucket = _t5_bucket(dil * np.arange(N_KEYS + 1))
        bias = jnp.take(rel_bias, bucket, axis=0)[:, g * N_HEADS:(g + 1) * N_HEADS].T
        bias = bias * LOG2E
        w = jnp.concatenate([bias[:, ::-1], jnp.full((N_HEADS, period - N_KEYS - 1), NEG, F32)], axis=1)
        band = jnp.tile(w, (1, QBLK))[:, :QBLK * (period - 1)]
        band = band.reshape(N_HEADS, QBLK, period - 1)[:, :, :2 * QBLK]
        band_first = jnp.where(ci >= QBLK, band, NEG)
        pair_rows = (N_HEADS // 2, 2 * QBLK, 2 * QBLK)
        tabs.append(jnp.stack([band.reshape(pair_rows), band_first.reshape(pair_rows)], axis=0))
        vals = bias[:, N_KEYS:0:-1][:, :, None]
        skipped = jnp.full((N_HEADS, N_KEYS, dil - 1), NEG, F32)
        bts.append(jnp.concatenate([vals, skipped], axis=2).reshape(N_HEADS, 1, win))
        b0s.append(bias[:, 0:1][:, :, None])
    return jnp.stack(tabs), bts, jnp.stack(b0s)


def kernel(x_prompt, x_sample, state_sc_conv, cache_kv_d1, cache_kv_d4, cache_kv_d16, state_ffn_conv,
           norm_mix, norm_ffn, sc_w_in, sc_conv_w, sc_w_out, attn_w_qkv, attn_q_norm, attn_k_norm,
           attn_w_out, rel_bias, ffn_w_gate, ffn_w_up, ffn_conv_w, ffn_conv_b, ffn_w_down):
    depth = norm_mix.shape[0]
    nb = x_sample.shape[0]
    caches_t = [jnp.transpose(c, (0, 1, 3, 4, 5, 2)) for c in (cache_kv_d1, cache_kv_d4, cache_kv_d16)]
    tables, bts, b0 = _bias_tables(rel_bias)
    seg = np.arange(ATTN_WIDTH) // HEAD_DIM
    bd = jnp.asarray(seg[:, None] == seg[None, :], BF)

    mixer_w = (sc_w_in.astype(BF), sc_conv_w, sc_w_out.astype(BF))
    w_qkv_b, w_ao_b = attn_w_qkv.astype(BF), attn_w_out.astype(BF)
    ffn_w = (norm_ffn[:, None, :], ffn_w_gate.astype(BF), ffn_w_up.astype(BF), ffn_conv_w,
             ffn_conv_b[:, None, :], ffn_w_down.astype(BF))
    g_mix = norm_mix[:, None, :]
    q_gain = (jnp.tile(attn_q_norm, (1, N_HEADS)) * (SCALE * LOG2E))[:, None, :]
    k_gain = jnp.tile(attn_k_norm, (1, N_HEADS))[:, None, :]

    xp = x_prompt
    xs = x_sample[:, 0, :]
    p_sc, s_sc, p_ffn, s_ffn = [], [], [], []
    p_kv = [[] for _ in range(N_GROUPS)]
    new_caches = None
    for i in range(depth):
        j = i // 2
        sf = state_ffn_conv[i]
        if i % 2 == 0:
            xp, hist = _mixer_prompt(xp, i, j, g_mix, *mixer_w)
            p_sc.append(hist)
            st = state_sc_conv[j]
            xs, u = _mixer_sample(xs, i, j, g_mix, *mixer_w, st[:, 0], st[:, 1])
            s_sc.append(jnp.stack([st[:, 1], u], axis=1))
            pre_p = pre_s = None
        else:
            qkv, kv_rows = _qkv_prompt(xp, i, j, g_mix, w_qkv_b, q_gain, k_gain, bd)
            for g in range(N_GROUPS):
                p_kv[g].append(kv_rows[g].reshape(kv_rows[g].shape[:2] + (2, N_HEADS, HEAD_DIM)))
            pre_p = (_attn_prompt(qkv, tables), w_ao_b, j)

            qs, kvs = _qkv_sample(xs, i, j, g_mix, w_qkv_b, q_gain, k_gain, bd)
            qs = jnp.swapaxes(qs.reshape(nb, N_GROUPS, N_HEADS, HEAD_DIM), -1, -2)
            kvs = jnp.swapaxes(kvs.reshape(nb, N_GROUPS, 2, N_HEADS, HEAD_DIM), -1, -2)
            xp, fh, a_s, *new_caches = _ffn_prompt(
                xp, pre_p, i, *ffn_w, sample=(qs, kvs, caches_t, j, bts, b0, new_caches))
            a_s = jnp.swapaxes(a_s.reshape(nb, -1, HEAD_DIM, a_s.shape[-1]), -1, -2)
            pre_s = (a_s.reshape(nb, ATTN_WIDTH), w_ao_b, j)
        if i % 2 == 0:
            xp, fh = _ffn_prompt(xp, pre_p, i, *ffn_w)
        p_ffn.append(fh)
        xs, gate = _ffn_sample(xs, pre_s, i, *ffn_w, sf[:, 0], sf[:, 1])
        s_ffn.append(jnp.stack([sf[:, 1], gate], axis=1))

    s_kv = [jnp.transpose(c, (0, 1, 5, 2, 3, 4)) for c in new_caches]
    return (xp, xs[:, None, :],
            jnp.stack(p_sc, axis=0), jnp.stack(p_kv[0], axis=0), jnp.stack(p_kv[1], axis=0),
            jnp.stack(p_kv[2], axis=0), jnp.stack(p_ffn, axis=0),
            jnp.stack(s_sc, axis=0), s_kv[0], s_kv[1], s_kv[2], jnp.stack(s_ffn, axis=0))
```

```python
import functools
import math

import numpy as np
import jax
import jax.numpy as jnp
from jax import lax
from jax.experimental import pallas as pl
from jax.experimental.pallas import tpu as pltpu

D_MODEL = 1024
D_FF = 2816
N_HEADS = 8
HEAD_DIM = 64
ATTN_WIDTH = N_HEADS * HEAD_DIM
GROUPS = ((128, 1), (512, 4), (2048, 16))
N_GROUPS = len(GROUPS)
N_KEYS = 128
N_BUCKETS = 32
MAX_DISTANCE = 2048
EPS = 1e-6
NEG = -1e30
SCALE = HEAD_DIM ** -0.5
LOG2E = math.log2(math.e)
SUPER = 2048
QBLK = 128
LANES = 128
ATTN_UNROLL = 16
BF = jnp.bfloat16
F32 = jnp.float32

V7X_VMEM_BYTES = 64 * 1024 * 1024
VMEM_LIMIT = V7X_VMEM_BYTES - 8 * 1024 * 1024


def _cparams(sem):
    return pltpu.CompilerParams(dimension_semantics=sem, vmem_limit_bytes=VMEM_LIMIT)


def _const_spec(shape):
    nd = len(shape)
    return pl.BlockSpec(shape, lambda *_: (0,) * nd, pipeline_mode=pl.Buffered(1))


def _whole_spec(shape):
    nd = len(shape)
    return pl.BlockSpec(shape, lambda *_: (0,) * nd)


def _layer_spec(arr, layer):
    nd = arr.ndim
    return pl.BlockSpec((None,) + arr.shape[1:], lambda *_: (layer,) + (0,) * (nd - 1),
                        pipeline_mode=pl.Buffered(1))


def _dot(a, b):
    return jnp.dot(a, b, preferred_element_type=F32)


def _rms(x, g):
    r = lax.rsqrt(jnp.mean(x * x, axis=-1, keepdims=True) + EPS)
    return (x * r) * g


def _silu(z):
    half = 0.5 * z
    return half + half * jnp.tanh(half)


def _head_rms(x, gain, bd):
    ss = _dot((x * x).astype(BF), bd)
    r = lax.rsqrt(ss * (1.0 / HEAD_DIM) + EPS)
    return (x * r) * gain


def _conv3(buf, cw_ref, tm):
    ext = buf[0:tm + 8, :]
    back2 = pltpu.roll(ext, 2, axis=0)[8:8 + tm, :]
    back1 = pltpu.roll(ext, 1, axis=0)[8:8 + tm, :]
    out = cw_ref[0:1, :] * back2
    out = out + cw_ref[1:2, :] * back1
    return out + cw_ref[2:3, :] * ext[8:8 + tm, :]


def _mixer_body(x_ref, g_ref, win_ref, cw_ref, wout_ref, o_ref, hist_ref, ubuf, *, tm):
    t = pl.program_id(1)

    @pl.when(t == 0)
    def _():
        ubuf[0:8, :] = jnp.zeros((8, D_MODEL), F32)

    x = x_ref[0]
    h = _rms(x, g_ref[...]).astype(BF)
    bg = _dot(h, win_ref[:, 0:D_MODEL])
    cg = _dot(h, win_ref[:, D_MODEL:2 * D_MODEL])
    xv = _dot(h, win_ref[:, 2 * D_MODEL:3 * D_MODEL])
    ubuf[8:8 + tm, :] = cg * xv
    conv = _conv3(ubuf, cw_ref, tm)
    y = _dot((bg * conv).astype(BF), wout_ref[...])
    o_ref[0] = x + y
    tail = ubuf[tm:tm + 8, :]
    ubuf[0:8, :] = tail
    hist_ref[0] = tail[6:8, :]


def _mixer_prompt(x, layer, mix_layer, g, w_in, conv_w, w_out, tm=512):
    b, t, d = x.shape
    nt = t // tm
    return pl.pallas_call(
        functools.partial(_mixer_body, tm=tm),
        grid=(b, nt),
        in_specs=[
            pl.BlockSpec((1, tm, d), lambda i, j: (i, j, 0)),
            _layer_spec(g, layer),
            *[_layer_spec(a, mix_layer) for a in (w_in, conv_w, w_out)],
        ],
        out_specs=[
            pl.BlockSpec((1, tm, d), lambda i, j: (i, j, 0)),
            pl.BlockSpec((1, 2, d), lambda i, j: (i, 0, 0)),
        ],
        out_shape=[
            jax.ShapeDtypeStruct((b, t, d), F32),
            jax.ShapeDtypeStruct((b, 2, d), F32),
        ],
        scratch_shapes=[pltpu.VMEM((tm + 8, d), F32)],
        compiler_params=_cparams(("arbitrary", "arbitrary")),
        name="mixer_prompt",
    )(x, g, w_in, conv_w, w_out)


def _sample_heads(first_head, q_ref, kvn_ref, c_refs, bt_refs, b0_ref, o_ref, n_refs):
    heads = o_ref.shape[-1]
    head_lane = lax.broadcasted_iota(jnp.int32, (HEAD_DIM, N_HEADS), 1)
    out_lane = lax.broadcasted_iota(jnp.int32, (HEAD_DIM, heads), 1)
    out = jnp.zeros((HEAD_DIM, heads), F32)
    for hh in range(heads):
        h = first_head + hh
        mine = head_lane == h
        column = lambda x: jnp.sum(jnp.where(mine, x, 0.0), axis=1, keepdims=True)
        ms, ss, nums = [], [], []
        for g in range(N_GROUPS):
            c_ref, n_ref = c_refs[g], n_refs[g]
            keep = c_ref.shape[-1]
            qc = column(q_ref[0, g])
            kn = column(kvn_ref[0, g, 0])
            vn = column(kvn_ref[0, g, 1])
            kt = c_ref[0, 0, 0, hh]
            vt = c_ref[0, 0, 1, hh]
            l = jnp.sum(kt * qc, axis=0, keepdims=True) + bt_refs[g][h]
            ln = jnp.sum(kn * qc, axis=0, keepdims=True) + b0_ref[g, h]
            m = jnp.maximum(jnp.max(l, axis=1, keepdims=True), ln)
            e = jnp.exp2(l - m)
            en = jnp.exp2(ln - m)
            ss.append(jnp.sum(e, axis=1, keepdims=True) + en)
            nums.append(jnp.sum(vt * e, axis=1, keepdims=True) + vn * en)
            ms.append(m)
            n_ref[0, 0, 0, hh] = pltpu.roll(kt, keep - 1, axis=1)
            n_ref[0, 0, 1, hh] = pltpu.roll(vt, keep - 1, axis=1)
            n_ref[0, 0, 0, hh, :, keep - 1:keep] = kn
            n_ref[0, 0, 1, hh, :, keep - 1:keep] = vn
        mm = jnp.maximum(jnp.maximum(ms[0], ms[1]), ms[2])
        w = [jnp.exp2(m - mm) for m in ms]
        den = w[0] * ss[0] + w[1] * ss[1] + w[2] * ss[2]
        num = w[0] * nums[0] + w[1] * nums[1] + w[2] * nums[2]
        out = jnp.where(out_lane == hh, num / den, out)
    o_ref[0] = out


N_SAMPLE_IN = 9


def _ffn_body(*refs, tm, nt, has_pre, n_sample_in, n_alias):
    if has_pre:
        x_ref, a_ref, wa_ref = refs[:3]
        refs = refs[3:]
    else:
        x_ref = refs[0]
        refs = refs[1:]
    g_ref, wg_ref, wu_ref, cw_ref, cb_ref, wd_ref = refs[:6]
    sample_in = refs[6:6 + n_sample_in]
    outs = refs[6 + n_sample_in + n_alias:-1]
    o_ref, hist_ref, sample_out = outs[0], outs[1], outs[2:]
    gbuf = refs[-1]
    t = pl.program_id(1)

    @pl.when(t == 0)
    def _():
        gbuf[0:8, :] = jnp.zeros((8, D_FF), F32)

    if n_sample_in:
        q_ref, kvn_ref = sample_in[0:2]
        heads = sample_out[0].shape[-1]
        step = pl.program_id(0) * nt + t
        _sample_heads((step % (N_HEADS // heads)) * heads, q_ref, kvn_ref, sample_in[2:5], sample_in[5:8],
                      sample_in[8], sample_out[0], sample_out[1:4])
    x = x_ref[0]
    if has_pre:
        x = x + _dot(a_ref[0].astype(BF), wa_ref[...])
    h = _rms(x, g_ref[...]).astype(BF)
    gbuf[8:8 + tm, :] = _dot(h, wg_ref[...])
    up = _dot(h, wu_ref[...])
    z = _conv3(gbuf, cw_ref, tm) + cb_ref[...]
    act = _silu(z) * up
    o_ref[0] = x + _dot(act.astype(BF), wd_ref[...])
    tail = gbuf[tm:tm + 8, :]
    gbuf[0:8, :] = tail
    hist_ref[0] = tail[6:8, :]


def _ffn_prompt(x, pre, layer, g, w_gate, w_up, conv_w, conv_b, w_down, sample=None):
    b, t, d = x.shape
    tm = 512 if sample is None else 256
    nt = t // tm
    row = lambda i, j: (i, j, 0)
    in_specs = [pl.BlockSpec((1, tm, d), row)]
    args = [x]
    if pre is not None:
        a, wa, pre_layer = pre
        in_specs += [pl.BlockSpec((1, tm, a.shape[-1]), row), _layer_spec(wa, pre_layer)]
        args += [a, wa]
    weights = (g, w_gate, w_up, conv_w, conv_b, w_down)
    in_specs += [_layer_spec(a, layer) for a in weights]
    args += weights
    out_specs = [
        pl.BlockSpec((1, tm, d), row),
        pl.BlockSpec((1, 2, D_FF), lambda i, j: (i, 0, 0)),
    ]
    out_shape = [
        jax.ShapeDtypeStruct((b, t, d), F32),
        jax.ShapeDtypeStruct((b, 2, D_FF), F32),
    ]
    aliases = {}
    if sample is not None:
        q, kvn, caches_t, attn_layer, bts, b0, prev = sample
        nb = q.shape[0]
        steps = b * nt
        parts = steps // nb
        heads = N_HEADS // parts
        assert parts * nb == steps and heads * parts == N_HEADS
        cache_specs = []
        for c, (win, dil) in zip(caches_t, GROUPS):
            assert c.shape[-1] == win, "the cache must hold one full window"
            cache_specs.append(pl.BlockSpec(
                (1, 1, 2, heads, HEAD_DIM, win),
                lambda i, j: (attn_layer, (i * nt + j) // parts, 0, (i * nt + j) % parts, 0, 0)))
        in_specs += [
            pl.BlockSpec((1,) + q.shape[1:], lambda i, j: ((i * nt + j) // parts, 0, 0, 0)),
            pl.BlockSpec((1,) + kvn.shape[1:], lambda i, j: ((i * nt + j) // parts, 0, 0, 0, 0)),
            *cache_specs,
            *[_const_spec(bt.shape) for bt in bts],
            _const_spec(b0.shape),
        ]
        args += [q, kvn, *caches_t, *bts, b0]
        if prev is not None:
            in_specs += [pl.BlockSpec(memory_space=pl.ANY)] * N_GROUPS
            aliases = {len(args) + k: 3 + k for k in range(N_GROUPS)}
            args += list(prev)
        out_specs += [pl.BlockSpec((1, HEAD_DIM, heads), lambda i, j: (i * nt + j, 0, 0)), *cache_specs]
        out_shape += [jax.ShapeDtypeStruct((steps, HEAD_DIM, heads), F32)]
        out_shape += [jax.ShapeDtypeStruct(c.shape, c.dtype) for c in caches_t]
    return pl.pallas_call(
        functools.partial(_ffn_body, tm=tm, nt=nt, has_pre=pre is not None,
                          n_sample_in=0 if sample is None else N_SAMPLE_IN, n_alias=len(aliases)),
        grid=(b, nt),
        in_specs=in_specs,
        out_specs=out_specs,
        out_shape=out_shape,
        input_output_aliases=aliases,
        scratch_shapes=[pltpu.VMEM((tm + 8, D_FF), F32)],
        compiler_params=_cparams(("arbitrary", "arbitrary")),
        name="ffn_prompt",
    )(*args)


def _qkv_body(x_ref, g_ref, w_ref, qg_ref, kg_ref, bd_ref, *refs, tm, n_alias):
    refs = refs[n_alias:]
    qkv_refs = refs[:3 * N_GROUPS]
    cache_refs = refs[3 * N_GROUPS:4 * N_GROUPS]
    h = _rms(x_ref[0], g_ref[...]).astype(BF)
    bd = bd_ref[...]
    for g, (win, dil) in reversed(list(enumerate(GROUPS))):
        c0 = g * 3 * ATTN_WIDTH
        q = _dot(h, w_ref[:, c0:c0 + ATTN_WIDTH])
        k = _dot(h, w_ref[:, c0 + ATTN_WIDTH:c0 + 2 * ATTN_WIDTH])
        v = _dot(h, w_ref[:, c0 + 2 * ATTN_WIDTH:c0 + 3 * ATTN_WIDTH])
        q = _head_rms(q, qg_ref[...], bd)
        k = _head_rms(k, kg_ref[...], bd)

        c_ref = cache_refs[g]
        rows = min(win, tm)
        c_ref[0, :, 0:ATTN_WIDTH] = k[tm - rows:, :]
        c_ref[0, :, ATTN_WIDTH:2 * ATTN_WIDTH] = v[tm - rows:, :]

        for o_ref, val in zip(qkv_refs[3 * g:3 * g + 3], (q, k, v)):
            if dil == 1:
                o_ref[0, 0, 0] = val.astype(BF)
            else:
                by_residue = pltpu.einshape("abc->bac", val.reshape(tm // dil, dil, ATTN_WIDTH))
                o_ref[0, 0] = by_residue.astype(BF)


def _qkv_prompt(x, layer, attn_layer, g, w_qkv, q_gain, k_gain, bd, prev_rows, tm=512):
    n_attn = w_qkv.shape[0]
    b, t, d = x.shape
    nt = t // tm
    tiles_per_super = SUPER // tm
    out_specs, out_shapes = [], []
    for win, dil in GROUPS:
        rows = tm // dil
        shp = (b, t // SUPER, dil, SUPER // dil, ATTN_WIDTH)
        spec = pl.BlockSpec(
            (1, 1, dil, rows, ATTN_WIDTH),
            lambda i, j: (i, j // tiles_per_super, 0, j % tiles_per_super, 0))
        out_specs += [spec] * 3
        out_shapes += [jax.ShapeDtypeStruct(shp, BF)] * 3
    for win, dil in GROUPS:
        keep = win
        if keep >= tm:
            first = nt - keep // tm
            spec = pl.BlockSpec((None, 1, tm, 2 * ATTN_WIDTH),
                                lambda i, j, first=first: (attn_layer, i, jnp.maximum(j - first, 0), 0))
        else:
            spec = pl.BlockSpec((None, 1, keep, 2 * ATTN_WIDTH), lambda i, j: (attn_layer, i, 0, 0))
        out_specs.append(spec)
        out_shapes.append(jax.ShapeDtypeStruct((n_attn, b, keep, 2 * ATTN_WIDTH), F32))
    n_in = 6
    aliases = {} if prev_rows is None else {n_in + k: 3 * N_GROUPS + k for k in range(N_GROUPS)}
    res = pl.pallas_call(
        functools.partial(_qkv_body, tm=tm, n_alias=len(aliases)),
        grid=(b, nt),
        in_specs=[
            pl.BlockSpec((1, tm, d), lambda i, j: (i, j, 0)),
            _layer_spec(g, layer),
            *[_layer_spec(a, attn_layer) for a in (w_qkv, q_gain, k_gain)],
            _const_spec((ATTN_WIDTH, ATTN_WIDTH)),
            *[pl.BlockSpec(memory_space=pl.ANY)] * len(aliases),
        ],
        out_specs=out_specs,
        out_shape=out_shapes,
        input_output_aliases=aliases,
        compiler_params=_cparams(("arbitrary", "arbitrary")),
        name="qkv_prompt",
    )(x, g, w_qkv, q_gain, k_gain, bd, *(prev_rows or ()))
    qkv = [r.reshape(b, t, ATTN_WIDTH) for r in res[:3 * N_GROUPS]]
    return qkv, res[3 * N_GROUPS:]


def _attn_body(*refs):
    qkv_refs = refs[:3 * N_GROUPS]
    tb_ref = refs[3 * N_GROUPS]
    o_ref = refs[3 * N_GROUPS + 1]
    out_s, lse_s = refs[3 * N_GROUPS + 2:]
    n_super = o_ref.shape[1] // SUPER
    blocks = SUPER // QBLK
    lane = lax.broadcasted_iota(jnp.int32, (QBLK, LANES), 1)
    first_head = lane < HEAD_DIM
    contract_last = (((1,), (1,)), ((), ()))
    zero = jnp.zeros((), BF)
    lane2 = lax.broadcasted_iota(jnp.int32, (2 * QBLK, LANES), 1)
    first_head2 = lane2 < HEAD_DIM
    ones_h0 = jnp.where(first_head2, 1.0, 0.0).astype(BF)
    ones_h1 = jnp.where(first_head2, 0.0, 1.0).astype(BF)

    for s in range(n_super):
        for g, (win, dil) in enumerate(GROUPS):
            q_ref, k_ref, v_ref = qkv_refs[3 * g:3 * g + 3]
            per_res = blocks // dil

            def trip(it, carry, s=s, g=g, dil=dil, per_res=per_res,
                     q_ref=q_ref, k_ref=k_ref, v_ref=v_ref):
                def logits_of(u):
                    n = it * ATTN_UNROLL + u
                    c = n % per_res
                    r = n // per_res
                    start = s * SUPER + n * QBLK
                    first = int(c == 0 and s == 0)
                    if c > 0:
                        prev = start - QBLK
                    elif s == 0:
                        prev = start
                    else:
                        prev = start - SUPER + SUPER // dil - QBLK
                    qb = q_ref[0, pl.ds(start, QBLK), :]
                    q2 = jnp.concatenate(
                        [jnp.where(first_head, qb, zero), jnp.where(first_head, zero, qb)], axis=0)
                    k2 = jnp.concatenate(
                        [k_ref[0, pl.ds(prev, QBLK), :], k_ref[0, pl.ds(start, QBLK), :]], axis=0)
                    logits = lax.dot_general(q2, k2, contract_last, preferred_element_type=F32)
                    return logits + tb_ref[g, first, 0], start, prev, r + dil * QBLK * c

                def probs_of(logits, start, prev, nat):
                    m = jnp.max(logits, axis=1, keepdims=True)
                    return jnp.exp2(logits - m).astype(BF), m, start, prev, nat

                def finish(e, m, start, prev, nat):
                    v2 = jnp.concatenate(
                        [v_ref[0, pl.ds(prev, QBLK), :], v_ref[0, pl.ds(start, QBLK), :]], axis=0)
                    rhs = jnp.concatenate([
                        jnp.concatenate([jnp.where(first_head2, v2, zero), ones_h0], axis=1),
                        jnp.concatenate([jnp.where(first_head2, zero, v2), ones_h1], axis=1)], axis=0)
                    pv = _dot(jnp.concatenate([e[0:QBLK], e[QBLK:]], axis=1), rhs)
                    rows = pl.ds(nat, QBLK) if dil == 1 else pl.ds(nat, QBLK, stride=dil)
                    row_sum = pv[:, LANES:]
                    out_s[s, g, rows, :] = pv[:, 0:LANES] / row_sum
                    lse_s[s, g, rows, :] = jnp.where(first_head, m[0:QBLK], m[QBLK:]) + jnp.log2(row_sum)

                staged, probs = {}, {}
                for u in range(ATTN_UNROLL + 2):
                    if u < ATTN_UNROLL:
                        staged[u] = logits_of(u)
                    if 1 <= u <= ATTN_UNROLL:
                        probs[u - 1] = probs_of(*staged.pop(u - 1))
                    if u >= 2:
                        finish(*probs.pop(u - 2))
                return carry

            for it in range(blocks // ATTN_UNROLL):
                trip(it, 0)

        mrows = 256
        for i in range(SUPER // mrows):
            rows = slice(i * mrows, (i + 1) * mrows)
            l0, l1, l2 = lse_s[s, 0, rows, :], lse_s[s, 1, rows, :], lse_s[s, 2, rows, :]
            top = jnp.maximum(jnp.maximum(l0, l1), l2)
            w0, w1, w2 = jnp.exp2(l0 - top), jnp.exp2(l1 - top), jnp.exp2(l2 - top)
            num = w0 * out_s[s, 0, rows, :] + w1 * out_s[s, 1, rows, :] + w2 * out_s[s, 2, rows, :]
            o_ref[0, s * SUPER + i * mrows:s * SUPER + (i + 1) * mrows, :] = num / (w0 + w1 + w2)


def _attn_prompt(qkv, tables):
    b, t, _ = qkv[0].shape
    pairs = N_HEADS // 2
    col = lambda i, j: (i, 0, j)
    return pl.pallas_call(
        _attn_body,
        grid=(b, pairs),
        in_specs=[pl.BlockSpec((1, t, LANES), col)] * (3 * N_GROUPS) + [
            pl.BlockSpec((N_GROUPS, 2, 1, 2 * QBLK, 2 * QBLK), lambda i, j: (0, 0, j, 0, 0)),
        ],
        out_specs=pl.BlockSpec((1, t, LANES), col),
        out_shape=jax.ShapeDtypeStruct((b, t, ATTN_WIDTH), F32),
        scratch_shapes=[pltpu.VMEM((t // SUPER, N_GROUPS, SUPER, LANES), F32)] * 2,
        compiler_params=_cparams(("arbitrary", "arbitrary")),
        name="attn_prompt",
    )(*qkv, tables)


def _mixer_sample_body(x_ref, g_ref, win_ref, cw_ref, wout_ref, h0_ref, h1_ref, o_ref, u_ref):
    x = x_ref[...]
    h = _rms(x, g_ref[...]).astype(BF)
    bg = _dot(h, win_ref[:, 0:D_MODEL])
    cg = _dot(h, win_ref[:, D_MODEL:2 * D_MODEL])
    xv = _dot(h, win_ref[:, 2 * D_MODEL:3 * D_MODEL])
    u = cg * xv
    conv = cw_ref[0:1, :] * h0_ref[...]
    conv = conv + cw_ref[1:2, :] * h1_ref[...]
    conv = conv + cw_ref[2:3, :] * u
    o_ref[...] = x + _dot((bg * conv).astype(BF), wout_ref[...])
    u_ref[...] = u


def _mixer_sample(x, layer, mix_layer, g, w_in, conv_w, w_out, h0, h1):
    n, d = x.shape
    return pl.pallas_call(
        _mixer_sample_body,
        grid=(1,),
        in_specs=[_const_spec(x.shape), _layer_spec(g, layer),
                  *[_layer_spec(a, mix_layer) for a in (w_in, conv_w, w_out)],
                  _const_spec(h0.shape), _const_spec(h1.shape)],
        out_specs=[_whole_spec((n, d))] * 2,
        out_shape=[jax.ShapeDtypeStruct((n, d), F32)] * 2,
        compiler_params=_cparams(("arbitrary",)),
        name="mixer_sample",
    )(x, g, w_in, conv_w, w_out, h0, h1)


def _ffn_sample_body(*refs, has_pre):
    if has_pre:
        x_ref, a_ref, wa_ref = refs[:3]
        refs = refs[3:]
    else:
        x_ref = refs[0]
        refs = refs[1:]
    g_ref, wg_ref, wu_ref, cw_ref, cb_ref, wd_ref, h0_ref, h1_ref, o_ref, gate_ref = refs
    x = x_ref[...]
    if has_pre:
        x = x + _dot(a_ref[...].astype(BF), wa_ref[...])
    h = _rms(x, g_ref[...]).astype(BF)
    gate = _dot(h, wg_ref[...])
    up = _dot(h, wu_ref[...])
    z = cw_ref[0:1, :] * h0_ref[...]
    z = z + cw_ref[1:2, :] * h1_ref[...]
    z = z + cw_ref[2:3, :] * gate
    z = z + cb_ref[...]
    act = _silu(z) * up
    o_ref[...] = x + _dot(act.astype(BF), wd_ref[...])
    gate_ref[...] = gate


def _ffn_sample(x, pre, layer, g, w_gate, w_up, conv_w, conv_b, w_down, h0, h1):
    n, d = x.shape
    args, in_specs = [x], [_const_spec(x.shape)]
    if pre is not None:
        a, wa, pre_layer = pre
        args += [a, wa]
        in_specs += [_const_spec(a.shape), _layer_spec(wa, pre_layer)]
    weights = (g, w_gate, w_up, conv_w, conv_b, w_down)
    args += [*weights, h0, h1]
    in_specs += [*[_layer_spec(a, layer) for a in weights], _const_spec(h0.shape), _const_spec(h1.shape)]
    return pl.pallas_call(
        functools.partial(_ffn_sample_body, has_pre=pre is not None),
        grid=(1,),
        in_specs=in_specs,
        out_specs=[_whole_spec((n, d)), _whole_spec((n, D_FF))],
        out_shape=[jax.ShapeDtypeStruct((n, d), F32), jax.ShapeDtypeStruct((n, D_FF), F32)],
        compiler_params=_cparams(("arbitrary",)),
        name="ffn_sample",
    )(*args)


def _qkv_sample_body(x_ref, g_ref, w_ref, qg_ref, kg_ref, bd_ref, q_ref, kv_ref):
    h = _rms(x_ref[...], g_ref[...]).astype(BF)
    bd = bd_ref[...]
    for g in range(N_GROUPS):
        c0 = g * 3 * ATTN_WIDTH
        q = _dot(h, w_ref[:, c0:c0 + ATTN_WIDTH])
        k = _dot(h, w_ref[:, c0 + ATTN_WIDTH:c0 + 2 * ATTN_WIDTH])
        v = _dot(h, w_ref[:, c0 + 2 * ATTN_WIDTH:c0 + 3 * ATTN_WIDTH])
        q_ref[:, g * ATTN_WIDTH:(g + 1) * ATTN_WIDTH] = _head_rms(q, qg_ref[...], bd)
        kv_ref[:, 2 * g * ATTN_WIDTH:(2 * g + 1) * ATTN_WIDTH] = _head_rms(k, kg_ref[...], bd)
        kv_ref[:, (2 * g + 1) * ATTN_WIDTH:(2 * g + 2) * ATTN_WIDTH] = v


def _qkv_sample(x, layer, attn_layer, g, w_qkv, q_gain, k_gain, bd):
    n = x.shape[0]
    out_cols = (N_GROUPS * ATTN_WIDTH, N_GROUPS * 2 * ATTN_WIDTH)
    return pl.pallas_call(
        _qkv_sample_body,
        grid=(1,),
        in_specs=[_const_spec(x.shape), _layer_spec(g, layer),
                  *[_layer_spec(a, attn_layer) for a in (w_qkv, q_gain, k_gain)], _const_spec(bd.shape)],
        out_specs=[_whole_spec((n, c)) for c in out_cols],
        out_shape=[jax.ShapeDtypeStruct((n, c), F32) for c in out_cols],
        compiler_params=_cparams(("arbitrary",)),
        name="qkv_sample",
    )(x, g, w_qkv, q_gain, k_gain, bd)


def _t5_bucket(dist):
    exact = N_BUCKETS // 2
    n = np.asarray(dist, dtype=np.float32)
    large = exact + np.log(np.maximum(n, 1.0) / exact) / math.log(MAX_DISTANCE / exact) * (N_BUCKETS - exact)
    large = np.minimum(np.floor(large), N_BUCKETS - 1)
    return np.where(n < exact, n, large).astype(np.int32)


def _bias_tables(rel_bias):
    ci = np.arange(2 * QBLK)[None, None, :]
    period = 3 * QBLK
    tabs, bts, b0s = [], [], []
    for g, (win, dil) in enumerate(GROUPS):
        bucket = _t5_bucket(dil * np.arange(N_KEYS + 1))
        bias = jnp.take(rel_bias, bucket, axis=0)[:, g * N_HEADS:(g + 1) * N_HEADS].T
        bias = bias * LOG2E
        w = jnp.concatenate([bias[:, ::-1], jnp.full((N_HEADS, period - N_KEYS - 1), NEG, F32)], axis=1)
        band = jnp.tile(w, (1, QBLK))[:, :QBLK * (period - 1)]
        band = band.reshape(N_HEADS, QBLK, period - 1)[:, :, :2 * QBLK]
        band_first = jnp.where(ci >= QBLK, band, NEG)
        pair_rows = (N_HEADS // 2, 2 * QBLK, 2 * QBLK)
        tabs.append(jnp.stack([band.reshape(pair_rows), band_first.reshape(pair_rows)], axis=0))
        vals = bias[:, N_KEYS:0:-1][:, :, None]
        skipped = jnp.full((N_HEADS, N_KEYS, dil - 1), NEG, F32)
        bts.append(jnp.concatenate([vals, skipped], axis=2).reshape(N_HEADS, 1, win))
        b0s.append(bias[:, 0:1][:, :, None])
    return jnp.stack(tabs), bts, jnp.stack(b0s)


def kernel(x_prompt, x_sample, state_sc_conv, cache_kv_d1, cache_kv_d4, cache_kv_d16, state_ffn_conv,
           norm_mix, norm_ffn, sc_w_in, sc_conv_w, sc_w_out, attn_w_qkv, attn_q_norm, attn_k_norm,
           attn_w_out, rel_bias, ffn_w_gate, ffn_w_up, ffn_conv_w, ffn_conv_b, ffn_w_down):
    depth = norm_mix.shape[0]
    nb = x_sample.shape[0]
    caches_t = [jnp.transpose(c, (0, 1, 3, 4, 5, 2)) for c in (cache_kv_d1, cache_kv_d4, cache_kv_d16)]
    tables, bts, b0 = _bias_tables(rel_bias)
    seg = np.arange(ATTN_WIDTH) // HEAD_DIM
    bd = jnp.asarray(seg[:, None] == seg[None, :], BF)

    mixer_w = (sc_w_in.astype(BF), sc_conv_w, sc_w_out.astype(BF))
    w_qkv_b, w_ao_b = attn_w_qkv.astype(BF), attn_w_out.astype(BF)
    ffn_w = (norm_ffn[:, None, :], ffn_w_gate.astype(BF), ffn_w_up.astype(BF), ffn_conv_w,
             ffn_conv_b[:, None, :], ffn_w_down.astype(BF))
    g_mix = norm_mix[:, None, :]
    q_gain = (jnp.tile(attn_q_norm, (1, N_HEADS)) * (SCALE * LOG2E))[:, None, :]
    k_gain = jnp.tile(attn_k_norm, (1, N_HEADS))[:, None, :]

    xp = x_prompt
    xs = x_sample[:, 0, :]
    p_sc, s_sc, p_ffn, s_ffn = [], [], [], []
    kv_rows = new_caches = None
    for i in range(depth):
        j = i // 2
        sf = state_ffn_conv[i]
        if i % 2 == 0:
            xp, hist = _mixer_prompt(xp, i, j, g_mix, *mixer_w)
            p_sc.append(hist)
            st = state_sc_conv[j]
            xs, u = _mixer_sample(xs, i, j, g_mix, *mixer_w, st[:, 0], st[:, 1])
            s_sc.append(jnp.stack([st[:, 1], u], axis=1))
            pre_p = pre_s = None
        else:
            qkv, kv_rows = _qkv_prompt(xp, i, j, g_mix, w_qkv_b, q_gain, k_gain, bd, kv_rows)
            pre_p = (_attn_prompt(qkv, tables), w_ao_b, j)

            qs, kvs = _qkv_sample(xs, i, j, g_mix, w_qkv_b, q_gain, k_gain, bd)
            qs = jnp.swapaxes(qs.reshape(nb, N_GROUPS, N_HEADS, HEAD_DIM), -1, -2)
            kvs = jnp.swapaxes(kvs.reshape(nb, N_GROUPS, 2, N_HEADS, HEAD_DIM), -1, -2)
            xp, fh, a_s, *new_caches = _ffn_prompt(
                xp, pre_p, i, *ffn_w, sample=(qs, kvs, caches_t, j, bts, b0, new_caches))
            a_s = jnp.swapaxes(a_s.reshape(nb, -1, HEAD_DIM, a_s.shape[-1]), -1, -2)
            pre_s = (a_s.reshape(nb, ATTN_WIDTH), w_ao_b, j)
        if i % 2 == 0:
            xp, fh = _ffn_prompt(xp, pre_p, i, *ffn_w)
        p_ffn.append(fh)
        xs, gate = _ffn_sample(xs, pre_s, i, *ffn_w, sf[:, 0], sf[:, 1])
        s_ffn.append(jnp.stack([sf[:, 1], gate], axis=1))

    s_kv = [jnp.transpose(c, (0, 1, 5, 2, 3, 4)) for c in new_caches]
    p_kv = [r.reshape(r.shape[:3] + (2, N_HEADS, HEAD_DIM)) for r in kv_rows]
    return (xp, xs[:, None, :],
            jnp.stack(p_sc, axis=0), p_kv[0], p_kv[1], p_kv[2], jnp.stack(p_ffn, axis=0),
            jnp.stack(s_sc, axis=0), s_kv[0], s_kv[1], s_kv[2], jnp.stack(s_ffn, axis=0))
```

```python
import functools
import math

import numpy as np
import jax
import jax.numpy as jnp
from jax import lax
from jax.experimental import pallas as pl
from jax.experimental.pallas import tpu as pltpu

D_MODEL = 1024
D_FF = 2816
N_HEADS = 8
HEAD_DIM = 64
ATTN_WIDTH = N_HEADS * HEAD_DIM
GROUPS = ((128, 1), (512, 4), (2048, 16))
N_GROUPS = len(GROUPS)
N_KEYS = 128
N_BUCKETS = 32
MAX_DISTANCE = 2048
EPS = 1e-6
NEG = -1e30
SCALE = HEAD_DIM ** -0.5
LOG2E = math.log2(math.e)
SUPER = 2048
QBLK = 128
LANES = 128
SUBLANES = 8
ATTN_UNROLL = 16
MERGE_ROWS = 256
MIXER_TILE = 1024
QKV_TILE = 512
FFN_TILE = 512
FFN_TILE_WITH_SAMPLE = 256
BF = jnp.bfloat16
F32 = jnp.float32

V7X_VMEM_BYTES = 64 * 1024 * 1024
VMEM_LIMIT = V7X_VMEM_BYTES - 8 * 1024 * 1024


def _cparams(sem):
    return pltpu.CompilerParams(dimension_semantics=sem, vmem_limit_bytes=VMEM_LIMIT)


def _const_spec(shape):
    nd = len(shape)
    return pl.BlockSpec(shape, lambda *_: (0,) * nd, pipeline_mode=pl.Buffered(1))


def _whole_spec(shape):
    nd = len(shape)
    return pl.BlockSpec(shape, lambda *_: (0,) * nd)


def _layer_spec(arr, layer):
    nd = arr.ndim
    return pl.BlockSpec((None,) + arr.shape[1:], lambda *_: (layer,) + (0,) * (nd - 1),
                        pipeline_mode=pl.Buffered(1))


def _dot(a, b):
    return jnp.dot(a, b, preferred_element_type=F32)


def _rms(x, g):
    r = lax.rsqrt(jnp.mean(x * x, axis=-1, keepdims=True) + EPS)
    return (x * r) * g


def _silu(z):
    half = 0.5 * z
    return half + half * jnp.tanh(half)


def _head_rms(x, gain, bd):
    ss = _dot((x * x).astype(BF), bd)
    r = lax.rsqrt(ss * (1.0 / HEAD_DIM) + EPS)
    return (x * r) * gain


def _conv3(buf, cw_ref, tm):
    ext = buf[0:tm + SUBLANES, :]
    back2 = pltpu.roll(ext, 2, axis=0)[SUBLANES:, :]
    back1 = pltpu.roll(ext, 1, axis=0)[SUBLANES:, :]
    out = cw_ref[0:1, :] * back2
    out = out + cw_ref[1:2, :] * back1
    return out + cw_ref[2:3, :] * ext[SUBLANES:, :]


def _carry_rows(buf, hist_ref, tm):
    tail = buf[tm:tm + SUBLANES, :]
    buf[0:SUBLANES, :] = tail
    hist_ref[0] = tail[SUBLANES - 2:, :]


def _mixer_body(x_ref, g_ref, win_ref, cw_ref, wout_ref, o_ref, hist_ref, ubuf, *, tm):
    t = pl.program_id(1)

    @pl.when(t == 0)
    def _():
        ubuf[0:SUBLANES, :] = jnp.zeros((SUBLANES, D_MODEL), F32)

    x = x_ref[0]
    h = _rms(x, g_ref[...]).astype(BF)
    bg = _dot(h, win_ref[:, 0:D_MODEL])
    cg = _dot(h, win_ref[:, D_MODEL:2 * D_MODEL])
    xv = _dot(h, win_ref[:, 2 * D_MODEL:3 * D_MODEL])
    ubuf[SUBLANES:SUBLANES + tm, :] = cg * xv
    conv = _conv3(ubuf, cw_ref, tm)
    y = _dot((bg * conv).astype(BF), wout_ref[...])
    o_ref[0] = x + y
    _carry_rows(ubuf, hist_ref, tm)


def _mixer_prompt(x, layer, mix_layer, g, w_in, conv_w, w_out, tm=MIXER_TILE):
    b, t, d = x.shape
    nt = t // tm
    return pl.pallas_call(
        functools.partial(_mixer_body, tm=tm),
        grid=(b, nt),
        in_specs=[
            pl.BlockSpec((1, tm, d), lambda i, j: (i, j, 0)),
            _layer_spec(g, layer),
            *[_layer_spec(a, mix_layer) for a in (w_in, conv_w, w_out)],
        ],
        out_specs=[
            pl.BlockSpec((1, tm, d), lambda i, j: (i, j, 0)),
            pl.BlockSpec((1, 2, d), lambda i, j: (i, 0, 0)),
        ],
        out_shape=[
            jax.ShapeDtypeStruct((b, t, d), F32),
            jax.ShapeDtypeStruct((b, 2, d), F32),
        ],
        scratch_shapes=[pltpu.VMEM((tm + SUBLANES, d), F32)],
        compiler_params=_cparams(("arbitrary", "arbitrary")),
        name="mixer_prompt",
    )(x, g, w_in, conv_w, w_out)


def _sample_heads(first_head, q_ref, kvn_ref, c_refs, bt_refs, b0_ref, o_ref, n_refs):
    heads = o_ref.shape[-1]
    head_lane = lax.broadcasted_iota(jnp.int32, (HEAD_DIM, N_HEADS), 1)
    out_lane = lax.broadcasted_iota(jnp.int32, (HEAD_DIM, heads), 1)
    out = jnp.zeros((HEAD_DIM, heads), F32)
    for hh in range(heads):
        h = first_head + hh
        mine = head_lane == h
        column = lambda x: jnp.sum(jnp.where(mine, x, 0.0), axis=1, keepdims=True)
        ms, ss, nums = [], [], []
        for g in range(N_GROUPS):
            c_ref, n_ref = c_refs[g], n_refs[g]
            keep = c_ref.shape[-1]
            qc = column(q_ref[0, g])
            kn = column(kvn_ref[0, g, 0])
            vn = column(kvn_ref[0, g, 1])
            kt = c_ref[0, 0, 0, hh]
            vt = c_ref[0, 0, 1, hh]
            l = jnp.sum(kt * qc, axis=0, keepdims=True) + bt_refs[g][h]
            ln = jnp.sum(kn * qc, axis=0, keepdims=True) + b0_ref[g, h]
            m = jnp.maximum(jnp.max(l, axis=1, keepdims=True), ln)
            e = jnp.exp2(l - m)
            en = jnp.exp2(ln - m)
            ss.append(jnp.sum(e, axis=1, keepdims=True) + en)
            nums.append(jnp.sum(vt * e, axis=1, keepdims=True) + vn * en)
            ms.append(m)
            n_ref[0, 0, 0, hh] = pltpu.roll(kt, keep - 1, axis=1)
            n_ref[0, 0, 1, hh] = pltpu.roll(vt, keep - 1, axis=1)
            n_ref[0, 0, 0, hh, :, keep - 1:keep] = kn
            n_ref[0, 0, 1, hh, :, keep - 1:keep] = vn
        mm = jnp.maximum(jnp.maximum(ms[0], ms[1]), ms[2])
        w = [jnp.exp2(m - mm) for m in ms]
        den = w[0] * ss[0] + w[1] * ss[1] + w[2] * ss[2]
        num = w[0] * nums[0] + w[1] * nums[1] + w[2] * nums[2]
        out = jnp.where(out_lane == hh, num / den, out)
    o_ref[0] = out


N_SAMPLE_IN = 9


def _ffn_body(*refs, tm, nt, has_pre, n_sample_in, n_alias):
    if has_pre:
        x_ref, a_ref, wa_ref = refs[:3]
        refs = refs[3:]
    else:
        x_ref = refs[0]
        refs = refs[1:]
    g_ref, wg_ref, wu_ref, cw_ref, cb_ref, wd_ref = refs[:6]
    sample_in = refs[6:6 + n_sample_in]
    outs = refs[6 + n_sample_in + n_alias:-1]
    o_ref, hist_ref, sample_out = outs[0], outs[1], outs[2:]
    gbuf = refs[-1]
    t = pl.program_id(1)

    @pl.when(t == 0)
    def _():
        gbuf[0:SUBLANES, :] = jnp.zeros((SUBLANES, D_FF), F32)

    if n_sample_in:
        q_ref, kvn_ref = sample_in[0:2]
        heads = sample_out[0].shape[-1]
        step = pl.program_id(0) * nt + t
        _sample_heads((step % (N_HEADS // heads)) * heads, q_ref, kvn_ref, sample_in[2:5], sample_in[5:8],
                      sample_in[8], sample_out[0], sample_out[1:4])
    x = x_ref[0]
    if has_pre:
        x = x + _dot(a_ref[0].astype(BF), wa_ref[...])
    h = _rms(x, g_ref[...]).astype(BF)
    gbuf[SUBLANES:SUBLANES + tm, :] = _dot(h, wg_ref[...])
    up = _dot(h, wu_ref[...])
    z = _conv3(gbuf, cw_ref, tm) + cb_ref[...]
    act = _silu(z) * up
    o_ref[0] = x + _dot(act.astype(BF), wd_ref[...])
    _carry_rows(gbuf, hist_ref, tm)


def _ffn_prompt(x, pre, layer, g, w_gate, w_up, conv_w, conv_b, w_down, sample=None):
    b, t, d = x.shape
    tm = FFN_TILE if sample is None else FFN_TILE_WITH_SAMPLE
    nt = t // tm
    row = lambda i, j: (i, j, 0)
    in_specs = [pl.BlockSpec((1, tm, d), row)]
    args = [x]
    if pre is not None:
        a, wa, pre_layer = pre
        in_specs += [pl.BlockSpec((1, tm, a.shape[-1]), row), _layer_spec(wa, pre_layer)]
        args += [a, wa]
    weights = (g, w_gate, w_up, conv_w, conv_b, w_down)
    in_specs += [_layer_spec(a, layer) for a in weights]
    args += weights
    out_specs = [
        pl.BlockSpec((1, tm, d), row),
        pl.BlockSpec((1, 2, D_FF), lambda i, j: (i, 0, 0)),
    ]
    out_shape = [
        jax.ShapeDtypeStruct((b, t, d), F32),
        jax.ShapeDtypeStruct((b, 2, D_FF), F32),
    ]
    aliases = {}
    if sample is not None:
        q, kvn, caches_t, attn_layer, bts, b0, prev = sample
        nb = q.shape[0]
        steps = b * nt
        parts = steps // nb
        heads = N_HEADS // parts
        assert parts * nb == steps and heads * parts == N_HEADS
        cache_specs = []
        for c, (win, dil) in zip(caches_t, GROUPS):
            assert c.shape[-1] == win, "the cache must hold one full window"
            cache_specs.append(pl.BlockSpec(
                (1, 1, 2, heads, HEAD_DIM, win),
                lambda i, j: (attn_layer, (i * nt + j) // parts, 0, (i * nt + j) % parts, 0, 0)))
        in_specs += [
            pl.BlockSpec((1,) + q.shape[1:], lambda i, j: ((i * nt + j) // parts, 0, 0, 0)),
            pl.BlockSpec((1,) + kvn.shape[1:], lambda i, j: ((i * nt + j) // parts, 0, 0, 0, 0)),
            *cache_specs,
            *[_const_spec(bt.shape) for bt in bts],
            _const_spec(b0.shape),
        ]
        args += [q, kvn, *caches_t, *bts, b0]
        if prev is not None:
            in_specs += [pl.BlockSpec(memory_space=pl.ANY)] * N_GROUPS
            aliases = {len(args) + k: 3 + k for k in range(N_GROUPS)}
            args += list(prev)
        out_specs += [pl.BlockSpec((1, HEAD_DIM, heads), lambda i, j: (i * nt + j, 0, 0)), *cache_specs]
        out_shape += [jax.ShapeDtypeStruct((steps, HEAD_DIM, heads), F32)]
        out_shape += [jax.ShapeDtypeStruct(c.shape, c.dtype) for c in caches_t]
    return pl.pallas_call(
        functools.partial(_ffn_body, tm=tm, nt=nt, has_pre=pre is not None,
                          n_sample_in=0 if sample is None else N_SAMPLE_IN, n_alias=len(aliases)),
        grid=(b, nt),
        in_specs=in_specs,
        out_specs=out_specs,
        out_shape=out_shape,
        input_output_aliases=aliases,
        scratch_shapes=[pltpu.VMEM((tm + SUBLANES, D_FF), F32)],
        compiler_params=_cparams(("arbitrary", "arbitrary")),
        name="ffn_prompt",
    )(*args)


def _qkv_body(x_ref, g_ref, w_ref, qg_ref, kg_ref, bd_ref, *refs, tm, n_alias):
    refs = refs[n_alias:]
    qkv_refs = refs[:3 * N_GROUPS]
    cache_refs = refs[3 * N_GROUPS:4 * N_GROUPS]
    h = _rms(x_ref[0], g_ref[...]).astype(BF)
    bd = bd_ref[...]
    for g, (win, dil) in reversed(list(enumerate(GROUPS))):
        c0 = g * 3 * ATTN_WIDTH
        q = _dot(h, w_ref[:, c0:c0 + ATTN_WIDTH])
        k = _dot(h, w_ref[:, c0 + ATTN_WIDTH:c0 + 2 * ATTN_WIDTH])
        v = _dot(h, w_ref[:, c0 + 2 * ATTN_WIDTH:c0 + 3 * ATTN_WIDTH])
        q = _head_rms(q, qg_ref[...], bd)
        k = _head_rms(k, kg_ref[...], bd)

        c_ref = cache_refs[g]
        rows = min(win, tm)
        c_ref[0, :, 0:ATTN_WIDTH] = k[tm - rows:, :]
        c_ref[0, :, ATTN_WIDTH:2 * ATTN_WIDTH] = v[tm - rows:, :]

        for o_ref, val in zip(qkv_refs[3 * g:3 * g + 3], (q, k, v)):
            if dil == 1:
                o_ref[0, 0, 0] = val.astype(BF)
            else:
                by_residue = pltpu.einshape("abc->bac", val.reshape(tm // dil, dil, ATTN_WIDTH))
                o_ref[0, 0] = by_residue.astype(BF)


def _qkv_prompt(x, layer, attn_layer, g, w_qkv, q_gain, k_gain, bd, prev_rows, tm=QKV_TILE):
    n_attn = w_qkv.shape[0]
    b, t, d = x.shape
    nt = t // tm
    tiles_per_super = SUPER // tm
    out_specs, out_shapes = [], []
    for win, dil in GROUPS:
        rows = tm // dil
        shp = (b, t // SUPER, dil, SUPER // dil, ATTN_WIDTH)
        spec = pl.BlockSpec(
            (1, 1, dil, rows, ATTN_WIDTH),
            lambda i, j: (i, j // tiles_per_super, 0, j % tiles_per_super, 0))
        out_specs += [spec] * 3
        out_shapes += [jax.ShapeDtypeStruct(shp, BF)] * 3
    for win, dil in GROUPS:
        keep = win
        if keep >= tm:
            first = nt - keep // tm
            spec = pl.BlockSpec((None, 1, tm, 2 * ATTN_WIDTH),
                                lambda i, j, first=first: (attn_layer, i, jnp.maximum(j - first, 0), 0))
        else:
            spec = pl.BlockSpec((None, 1, keep, 2 * ATTN_WIDTH), lambda i, j: (attn_layer, i, 0, 0))
        out_specs.append(spec)
        out_shapes.append(jax.ShapeDtypeStruct((n_attn, b, keep, 2 * ATTN_WIDTH), F32))
    n_in = 6
    aliases = {} if prev_rows is None else {n_in + k: 3 * N_GROUPS + k for k in range(N_GROUPS)}
    res = pl.pallas_call(
        functools.partial(_qkv_body, tm=tm, n_alias=len(aliases)),
        grid=(b, nt),
        in_specs=[
            pl.BlockSpec((1, tm, d), lambda i, j: (i, j, 0)),
            _layer_spec(g, layer),
            *[_layer_spec(a, attn_layer) for a in (w_qkv, q_gain, k_gain)],
            _const_spec((ATTN_WIDTH, ATTN_WIDTH)),
            *[pl.BlockSpec(memory_space=pl.ANY)] * len(aliases),
        ],
        out_specs=out_specs,
        out_shape=out_shapes,
        input_output_aliases=aliases,
        compiler_params=_cparams(("arbitrary", "arbitrary")),
        name="qkv_prompt",
    )(x, g, w_qkv, q_gain, k_gain, bd, *(prev_rows or ()))
    qkv = [r.reshape(b, t, ATTN_WIDTH) for r in res[:3 * N_GROUPS]]
    return qkv, res[3 * N_GROUPS:]


def _attn_body(*refs):
    qkv_refs = refs[:3 * N_GROUPS]
    tb_ref = refs[3 * N_GROUPS]
    o_ref = refs[3 * N_GROUPS + 1]
    out_s, lse_s = refs[3 * N_GROUPS + 2:]
    n_super = o_ref.shape[1] // SUPER
    blocks = SUPER // QBLK
    lane = lax.broadcasted_iota(jnp.int32, (QBLK, LANES), 1)
    first_head = lane < HEAD_DIM
    contract_last = (((1,), (1,)), ((), ()))
    zero = jnp.zeros((), BF)
    lane2 = lax.broadcasted_iota(jnp.int32, (2 * QBLK, LANES), 1)
    first_head2 = lane2 < HEAD_DIM
    ones_h0 = jnp.where(first_head2, 1.0, 0.0).astype(BF)
    ones_h1 = jnp.where(first_head2, 0.0, 1.0).astype(BF)

    for s in range(n_super):
        for g, (win, dil) in enumerate(GROUPS):
            q_ref, k_ref, v_ref = qkv_refs[3 * g:3 * g + 3]
            per_res = blocks // dil

            def trip(it, carry, s=s, g=g, dil=dil, per_res=per_res,
                     q_ref=q_ref, k_ref=k_ref, v_ref=v_ref):
                def logits_of(u):
                    n = it * ATTN_UNROLL + u
                    c = n % per_res
                    r = n // per_res
                    start = s * SUPER + n * QBLK
                    first = int(c == 0 and s == 0)
                    if c > 0:
                        prev = start - QBLK
                    elif s == 0:
                        prev = start
                    else:
                        prev = start - SUPER + SUPER // dil - QBLK
                    qb = q_ref[0, pl.ds(start, QBLK), :]
                    q2 = jnp.concatenate(
                        [jnp.where(first_head, qb, zero), jnp.where(first_head, zero, qb)], axis=0)
                    k2 = jnp.concatenate(
                        [k_ref[0, pl.ds(prev, QBLK), :], k_ref[0, pl.ds(start, QBLK), :]], axis=0)
                    logits = lax.dot_general(q2, k2, contract_last, preferred_element_type=F32)
                    return logits + tb_ref[g, first, 0], start, prev, r + dil * QBLK * c

                def probs_of(logits, start, prev, nat):
                    m = jnp.max(logits, axis=1, keepdims=True)
                    return jnp.exp2(logits - m).astype(BF), m, start, prev, nat

                def finish(e, m, start, prev, nat):
                    v2 = jnp.concatenate(
                        [v_ref[0, pl.ds(prev, QBLK), :], v_ref[0, pl.ds(start, QBLK), :]], axis=0)
                    rhs = jnp.concatenate([
                        jnp.concatenate([jnp.where(first_head2, v2, zero), ones_h0], axis=1),
                        jnp.concatenate([jnp.where(first_head2, zero, v2), ones_h1], axis=1)], axis=0)
                    pv = _dot(jnp.concatenate([e[0:QBLK], e[QBLK:]], axis=1), rhs)
                    rows = pl.ds(nat, QBLK) if dil == 1 else pl.ds(nat, QBLK, stride=dil)
                    row_sum = pv[:, LANES:]
                    out_s[s, g, rows, :] = pv[:, 0:LANES] / row_sum
                    lse_s[s, g, rows, :] = jnp.where(first_head, m[0:QBLK], m[QBLK:]) + jnp.log2(row_sum)

                staged, probs = {}, {}
                for u in range(ATTN_UNROLL + 2):
                    if u < ATTN_UNROLL:
                        staged[u] = logits_of(u)
                    if 1 <= u <= ATTN_UNROLL:
                        probs[u - 1] = probs_of(*staged.pop(u - 1))
                    if u >= 2:
                        finish(*probs.pop(u - 2))
                return carry

            for it in range(blocks // ATTN_UNROLL):
                trip(it, 0)

        for i in range(SUPER // MERGE_ROWS):
            rows = slice(i * MERGE_ROWS, (i + 1) * MERGE_ROWS)
            l0, l1, l2 = lse_s[s, 0, rows, :], lse_s[s, 1, rows, :], lse_s[s, 2, rows, :]
            top = jnp.maximum(jnp.maximum(l0, l1), l2)
            w0, w1, w2 = jnp.exp2(l0 - top), jnp.exp2(l1 - top), jnp.exp2(l2 - top)
            num = w0 * out_s[s, 0, rows, :] + w1 * out_s[s, 1, rows, :] + w2 * out_s[s, 2, rows, :]
            o_ref[0, s * SUPER + i * MERGE_ROWS:s * SUPER + (i + 1) * MERGE_ROWS, :] = num / (w0 + w1 + w2)


def _attn_prompt(qkv, tables):
    b, t, _ = qkv[0].shape
    pairs = N_HEADS // 2
    col = lambda i, j: (i, 0, j)
    return pl.pallas_call(
        _attn_body,
        grid=(b, pairs),
        in_specs=[pl.BlockSpec((1, t, LANES), col)] * (3 * N_GROUPS) + [
            pl.BlockSpec((N_GROUPS, 2, 1, 2 * QBLK, 2 * QBLK), lambda i, j: (0, 0, j, 0, 0)),
        ],
        out_specs=pl.BlockSpec((1, t, LANES), col),
        out_shape=jax.ShapeDtypeStruct((b, t, ATTN_WIDTH), F32),
        scratch_shapes=[pltpu.VMEM((t // SUPER, N_GROUPS, SUPER, LANES), F32)] * 2,
        compiler_params=_cparams(("arbitrary", "arbitrary")),
        name="attn_prompt",
    )(*qkv, tables)


def _mixer_sample_body(x_ref, g_ref, win_ref, cw_ref, wout_ref, h0_ref, h1_ref, o_ref, u_ref):
    x = x_ref[...]
    h = _rms(x, g_ref[...]).astype(BF)
    bg = _dot(h, win_ref[:, 0:D_MODEL])
    cg = _dot(h, win_ref[:, D_MODEL:2 * D_MODEL])
    xv = _dot(h, win_ref[:, 2 * D_MODEL:3 * D_MODEL])
    u = cg * xv
    conv = cw_ref[0:1, :] * h0_ref[...]
    conv = conv + cw_ref[1:2, :] * h1_ref[...]
    conv = conv + cw_ref[2:3, :] * u
    o_ref[...] = x + _dot((bg * conv).astype(BF), wout_ref[...])
    u_ref[...] = u


def _mixer_sample(x, layer, mix_layer, g, w_in, conv_w, w_out, h0, h1):
    n, d = x.shape
    return pl.pallas_call(
        _mixer_sample_body,
        grid=(1,),
        in_specs=[_const_spec(x.shape), _layer_spec(g, layer),
                  *[_layer_spec(a, mix_layer) for a in (w_in, conv_w, w_out)],
                  _const_spec(h0.shape), _const_spec(h1.shape)],
        out_specs=[_whole_spec((n, d))] * 2,
        out_shape=[jax.ShapeDtypeStruct((n, d), F32)] * 2,
        compiler_params=_cparams(("arbitrary",)),
        name="mixer_sample",
    )(x, g, w_in, conv_w, w_out, h0, h1)


def _ffn_sample_body(*refs, has_pre):
    if has_pre:
        x_ref, a_ref, wa_ref = refs[:3]
        refs = refs[3:]
    else:
        x_ref = refs[0]
        refs = refs[1:]
    g_ref, wg_ref, wu_ref, cw_ref, cb_ref, wd_ref, h0_ref, h1_ref, o_ref, gate_ref = refs
    x = x_ref[...]
    if has_pre:
        x = x + _dot(a_ref[...].astype(BF), wa_ref[...])
    h = _rms(x, g_ref[...]).astype(BF)
    gate = _dot(h, wg_ref[...])
    up = _dot(h, wu_ref[...])
    z = cw_ref[0:1, :] * h0_ref[...]
    z = z + cw_ref[1:2, :] * h1_ref[...]
    z = z + cw_ref[2:3, :] * gate
    z = z + cb_ref[...]
    act = _silu(z) * up
    o_ref[...] = x + _dot(act.astype(BF), wd_ref[...])
    gate_ref[...] = gate


def _ffn_sample(x, pre, layer, g, w_gate, w_up, conv_w, conv_b, w_down, h0, h1):
    n, d = x.shape
    args, in_specs = [x], [_const_spec(x.shape)]
    if pre is not None:
        a, wa, pre_layer = pre
        args += [a, wa]
        in_specs += [_const_spec(a.shape), _layer_spec(wa, pre_layer)]
    weights = (g, w_gate, w_up, conv_w, conv_b, w_down)
    args += [*weights, h0, h1]
    in_specs += [*[_layer_spec(a, layer) for a in weights], _const_spec(h0.shape), _const_spec(h1.shape)]
    return pl.pallas_call(
        functools.partial(_ffn_sample_body, has_pre=pre is not None),
        grid=(1,),
        in_specs=in_specs,
        out_specs=[_whole_spec((n, d)), _whole_spec((n, D_FF))],
        out_shape=[jax.ShapeDtypeStruct((n, d), F32), jax.ShapeDtypeStruct((n, D_FF), F32)],
        compiler_params=_cparams(("arbitrary",)),
        name="ffn_sample",
    )(*args)


def _qkv_sample_body(x_ref, g_ref, w_ref, qg_ref, kg_ref, bd_ref, q_ref, kv_ref):
    h = _rms(x_ref[...], g_ref[...]).astype(BF)
    bd = bd_ref[...]
    for g in range(N_GROUPS):
        c0 = g * 3 * ATTN_WIDTH
        q = _dot(h, w_ref[:, c0:c0 + ATTN_WIDTH])
        k = _dot(h, w_ref[:, c0 + ATTN_WIDTH:c0 + 2 * ATTN_WIDTH])
        v = _dot(h, w_ref[:, c0 + 2 * ATTN_WIDTH:c0 + 3 * ATTN_WIDTH])
        q_ref[:, g * ATTN_WIDTH:(g + 1) * ATTN_WIDTH] = _head_rms(q, qg_ref[...], bd)
        kv_ref[:, 2 * g * ATTN_WIDTH:(2 * g + 1) * ATTN_WIDTH] = _head_rms(k, kg_ref[...], bd)
        kv_ref[:, (2 * g + 1) * ATTN_WIDTH:(2 * g + 2) * ATTN_WIDTH] = v


def _qkv_sample(x, layer, attn_layer, g, w_qkv, q_gain, k_gain, bd):
    n = x.shape[0]
    out_cols = (N_GROUPS * ATTN_WIDTH, N_GROUPS * 2 * ATTN_WIDTH)
    return pl.pallas_call(
        _qkv_sample_body,
        grid=(1,),
        in_specs=[_const_spec(x.shape), _layer_spec(g, layer),
                  *[_layer_spec(a, attn_layer) for a in (w_qkv, q_gain, k_gain)], _const_spec(bd.shape)],
        out_specs=[_whole_spec((n, c)) for c in out_cols],
        out_shape=[jax.ShapeDtypeStruct((n, c), F32) for c in out_cols],
        compiler_params=_cparams(("arbitrary",)),
        name="qkv_sample",
    )(x, g, w_qkv, q_gain, k_gain, bd)


def _t5_bucket(dist):
    exact = N_BUCKETS // 2
    n = np.asarray(dist, dtype=np.float32)
    large = exact + np.log(np.maximum(n, 1.0) / exact) / math.log(MAX_DISTANCE / exact) * (N_BUCKETS - exact)
    large = np.minimum(np.floor(large), N_BUCKETS - 1)
    return np.where(n < exact, n, large).astype(np.int32)


def _bias_tables(rel_bias):
    ci = np.arange(2 * QBLK)[None, None, :]
    period = 3 * QBLK
    tabs, bts, b0s = [], [], []
    for g, (win, dil) in enumerate(GROUPS):
        bucket = _t5_bucket(dil * np.arange(N_KEYS + 1))
        bias = jnp.take(rel_bias, bucket, axis=0)[:, g * N_HEADS:(g + 1) * N_HEADS].T
        bias = bias * LOG2E
        w = jnp.concatenate([bias[:, ::-1], jnp.full((N_HEADS, period - N_KEYS - 1), NEG, F32)], axis=1)
        band = jnp.tile(w, (1, QBLK))[:, :QBLK * (period - 1)]
        band = band.reshape(N_HEADS, QBLK, period - 1)[:, :, :2 * QBLK]
        band_first = jnp.where(ci >= QBLK, band, NEG)
        pair_rows = (N_HEADS // 2, 2 * QBLK, 2 * QBLK)
        tabs.append(jnp.stack([band.reshape(pair_rows), band_first.reshape(pair_rows)], axis=0))
        vals = bias[:, N_KEYS:0:-1][:, :, None]
        skipped = jnp.full((N_HEADS, N_KEYS, dil - 1), NEG, F32)
        bts.append(jnp.concatenate([vals, skipped], axis=2).reshape(N_HEADS, 1, win))
        b0s.append(bias[:, 0:1][:, :, None])
    return jnp.stack(tabs), bts, jnp.stack(b0s)


def kernel(x_prompt, x_sample, state_sc_conv, cache_kv_d1, cache_kv_d4, cache_kv_d16, state_ffn_conv,
           norm_mix, norm_ffn, sc_w_in, sc_conv_w, sc_w_out, attn_w_qkv, attn_q_norm, attn_k_norm,
           attn_w_out, rel_bias, ffn_w_gate, ffn_w_up, ffn_conv_w, ffn_conv_b, ffn_w_down):
    depth = norm_mix.shape[0]
    nb = x_sample.shape[0]
    caches_t = [jnp.transpose(c, (0, 1, 3, 4, 5, 2)) for c in (cache_kv_d1, cache_kv_d4, cache_kv_d16)]
    tables, bts, b0 = _bias_tables(rel_bias)
    seg = np.arange(ATTN_WIDTH) // HEAD_DIM
    bd = jnp.asarray(seg[:, None] == seg[None, :], BF)

    mixer_w = (sc_w_in.astype(BF), sc_conv_w, sc_w_out.astype(BF))
    w_qkv_b, w_ao_b = attn_w_qkv.astype(BF), attn_w_out.astype(BF)
    ffn_w = (norm_ffn[:, None, :], ffn_w_gate.astype(BF), ffn_w_up.astype(BF), ffn_conv_w,
             ffn_conv_b[:, None, :], ffn_w_down.astype(BF))
    g_mix = norm_mix[:, None, :]
    q_gain = (jnp.tile(attn_q_norm, (1, N_HEADS)) * (SCALE * LOG2E))[:, None, :]
    k_gain = jnp.tile(attn_k_norm, (1, N_HEADS))[:, None, :]

    xp = x_prompt
    xs = x_sample[:, 0, :]
    p_sc, s_sc, p_ffn, s_ffn = [], [], [], []
    kv_rows = new_caches = None
    for i in range(depth):
        j = i // 2
        sf = state_ffn_conv[i]
        if i % 2 == 0:
            xp, hist = _mixer_prompt(xp, i, j, g_mix, *mixer_w)
            p_sc.append(hist)
            st = state_sc_conv[j]
            xs, u = _mixer_sample(xs, i, j, g_mix, *mixer_w, st[:, 0], st[:, 1])
            s_sc.append(jnp.stack([st[:, 1], u], axis=1))
            pre_p = pre_s = None
        else:
            qkv, kv_rows = _qkv_prompt(xp, i, j, g_mix, w_qkv_b, q_gain, k_gain, bd, kv_rows)
            pre_p = (_attn_prompt(qkv, tables), w_ao_b, j)

            qs, kvs = _qkv_sample(xs, i, j, g_mix, w_qkv_b, q_gain, k_gain, bd)
            qs = jnp.swapaxes(qs.reshape(nb, N_GROUPS, N_HEADS, HEAD_DIM), -1, -2)
            kvs = jnp.swapaxes(kvs.reshape(nb, N_GROUPS, 2, N_HEADS, HEAD_DIM), -1, -2)
            xp, fh, a_s, *new_caches = _ffn_prompt(
                xp, pre_p, i, *ffn_w, sample=(qs, kvs, caches_t, j, bts, b0, new_caches))
            a_s = jnp.swapaxes(a_s.reshape(nb, -1, HEAD_DIM, a_s.shape[-1]), -1, -2)
            pre_s = (a_s.reshape(nb, ATTN_WIDTH), w_ao_b, j)
        if i % 2 == 0:
            xp, fh = _ffn_prompt(xp, pre_p, i, *ffn_w)
        p_ffn.append(fh)
        xs, gate = _ffn_sample(xs, pre_s, i, *ffn_w, sf[:, 0], sf[:, 1])
        s_ffn.append(jnp.stack([sf[:, 1], gate], axis=1))

    s_kv = [jnp.transpose(c, (0, 1, 5, 2, 3, 4)) for c in new_caches]
    p_kv = [r.reshape(r.shape[:3] + (2, N_HEADS, HEAD_DIM)) for r in kv_rows]
    return (xp, xs[:, None, :],
            jnp.stack(p_sc, axis=0), p_kv[0], p_kv[1], p_kv[2], jnp.stack(p_ffn, axis=0),
            jnp.stack(s_sc, axis=0), s_kv[0], s_kv[1], s_kv[2], jnp.stack(s_ffn, axis=0))
```

```python
import functools
import math

import numpy as np
import jax
import jax.numpy as jnp
from jax import lax
from jax.experimental import pallas as pl
from jax.experimental.pallas import tpu as pltpu

D_MODEL = 1024
D_FF = 2816
N_HEADS = 8
HEAD_DIM = 64
ATTN_WIDTH = N_HEADS * HEAD_DIM
GROUPS = ((128, 1), (512, 4), (2048, 16))
N_GROUPS = len(GROUPS)
N_KEYS = 128
N_BUCKETS = 32
MAX_DISTANCE = 2048
EPS = 1e-6
NEG = -1e30
SCALE = HEAD_DIM ** -0.5
LOG2E = math.log2(math.e)
SUPER = 2048
QBLK = 128
LANES = 128
SUBLANES = 8
ATTN_UNROLL = 16
MERGE_ROWS = 256
MIXER_TILE = 1024
QKV_TILE = 512
FFN_TILE = 512
FFN_TILE_WITH_SAMPLE = 256
BF = jnp.bfloat16
F32 = jnp.float32

V7X_VMEM_BYTES = 64 * 1024 * 1024
VMEM_LIMIT = V7X_VMEM_BYTES - 8 * 1024 * 1024


def _cparams(sem):
    return pltpu.CompilerParams(dimension_semantics=sem, vmem_limit_bytes=VMEM_LIMIT)


def _const_spec(shape):
    nd = len(shape)
    return pl.BlockSpec(shape, lambda *_: (0,) * nd, pipeline_mode=pl.Buffered(1))


def _whole_spec(shape):
    nd = len(shape)
    return pl.BlockSpec(shape, lambda *_: (0,) * nd)


def _layer_spec(arr, layer):
    nd = arr.ndim
    return pl.BlockSpec((None,) + arr.shape[1:], lambda *_: (layer,) + (0,) * (nd - 1),
                        pipeline_mode=pl.Buffered(1))


def _dot(a, b):
    return jnp.dot(a, b, preferred_element_type=F32)


def _rms(x, g):
    r = lax.rsqrt(jnp.mean(x * x, axis=-1, keepdims=True) + EPS)
    return (x * r) * g


def _silu(z):
    half = 0.5 * z
    return half + half * jnp.tanh(half)


def _head_rms(x, gain, bd):
    ss = _dot((x * x).astype(BF), bd)
    r = lax.rsqrt(ss * (1.0 / HEAD_DIM) + EPS)
    return (x * r) * gain


def _conv3(buf, cw_ref, tm):
    ext = buf[0:tm + SUBLANES, :]
    back2 = pltpu.roll(ext, 2, axis=0)[SUBLANES:, :]
    back1 = pltpu.roll(ext, 1, axis=0)[SUBLANES:, :]
    out = cw_ref[0:1, :] * back2
    out = out + cw_ref[1:2, :] * back1
    return out + cw_ref[2:3, :] * ext[SUBLANES:, :]


def _carry_rows(buf, hist_ref, tm):
    tail = buf[tm:tm + SUBLANES, :]
    buf[0:SUBLANES, :] = tail
    hist_ref[0] = tail[SUBLANES - 2:, :]


def _zero_fill_outputs(zero_fill, nt, steps):
    specs, shapes = [], []
    for shape, block in zero_fill:
        counts = [s // blk for s, blk in zip(shape, block)]
        assert math.prod(counts) == steps, "one block of every zero-filled buffer per grid step"

        def block_of_step(i, j, counts=counts):
            step, idx = i * nt + j, []
            for c in reversed(counts):
                idx.append(step % c)
                step = step // c
            return tuple(reversed(idx))

        specs.append(pl.BlockSpec(block, block_of_step))
        shapes.append(jax.ShapeDtypeStruct(shape, F32))
    return specs, shapes


def _mixer_body(x_ref, g_ref, win_ref, cw_ref, wout_ref, o_ref, hist_ref, ubuf, *, tm):
    t = pl.program_id(1)

    @pl.when(t == 0)
    def _():
        ubuf[0:SUBLANES, :] = jnp.zeros((SUBLANES, D_MODEL), F32)

    x = x_ref[0]
    h = _rms(x, g_ref[...]).astype(BF)
    bg = _dot(h, win_ref[:, 0:D_MODEL])
    cg = _dot(h, win_ref[:, D_MODEL:2 * D_MODEL])
    xv = _dot(h, win_ref[:, 2 * D_MODEL:3 * D_MODEL])
    ubuf[SUBLANES:SUBLANES + tm, :] = cg * xv
    conv = _conv3(ubuf, cw_ref, tm)
    y = _dot((bg * conv).astype(BF), wout_ref[...])
    o_ref[0] = x + y
    _carry_rows(ubuf, hist_ref, tm)


def _mixer_prompt(x, layer, mix_layer, g, w_in, conv_w, w_out, tm=MIXER_TILE):
    b, t, d = x.shape
    nt = t // tm
    return pl.pallas_call(
        functools.partial(_mixer_body, tm=tm),
        grid=(b, nt),
        in_specs=[
            pl.BlockSpec((1, tm, d), lambda i, j: (i, j, 0)),
            _layer_spec(g, layer),
            *[_layer_spec(a, mix_layer) for a in (w_in, conv_w, w_out)],
        ],
        out_specs=[
            pl.BlockSpec((1, tm, d), lambda i, j: (i, j, 0)),
            pl.BlockSpec((1, 2, d), lambda i, j: (i, 0, 0)),
        ],
        out_shape=[
            jax.ShapeDtypeStruct((b, t, d), F32),
            jax.ShapeDtypeStruct((b, 2, d), F32),
        ],
        scratch_shapes=[pltpu.VMEM((tm + SUBLANES, d), F32)],
        compiler_params=_cparams(("arbitrary", "arbitrary")),
        name="mixer_prompt",
    )(x, g, w_in, conv_w, w_out)


def _sample_heads(first_head, q_ref, kvn_ref, c_refs, bt_refs, b0_ref, o_ref, n_refs):
    heads = o_ref.shape[-1]
    head_lane = lax.broadcasted_iota(jnp.int32, (HEAD_DIM, N_HEADS), 1)
    out_lane = lax.broadcasted_iota(jnp.int32, (HEAD_DIM, heads), 1)
    out = jnp.zeros((HEAD_DIM, heads), F32)
    for hh in range(heads):
        h = first_head + hh
        mine = head_lane == h
        column = lambda x: jnp.sum(jnp.where(mine, x, 0.0), axis=1, keepdims=True)
        ms, ss, nums = [], [], []
        for g in range(N_GROUPS):
            c_ref, n_ref = c_refs[g], n_refs[g]
            keep = c_ref.shape[-1]
            qc = column(q_ref[0, g])
            kn = column(kvn_ref[0, g, 0])
            vn = column(kvn_ref[0, g, 1])
            kt = c_ref[0, 0, 0, hh]
            vt = c_ref[0, 0, 1, hh]
            l = jnp.sum(kt * qc, axis=0, keepdims=True) + bt_refs[g][h]
            ln = jnp.sum(kn * qc, axis=0, keepdims=True) + b0_ref[g, h]
            m = jnp.maximum(jnp.max(l, axis=1, keepdims=True), ln)
            e = jnp.exp2(l - m)
            en = jnp.exp2(ln - m)
            ss.append(jnp.sum(e, axis=1, keepdims=True) + en)
            nums.append(jnp.sum(vt * e, axis=1, keepdims=True) + vn * en)
            ms.append(m)
            n_ref[0, 0, 0, hh] = pltpu.roll(kt, keep - 1, axis=1)
            n_ref[0, 0, 1, hh] = pltpu.roll(vt, keep - 1, axis=1)
            n_ref[0, 0, 0, hh, :, keep - 1:keep] = kn
            n_ref[0, 0, 1, hh, :, keep - 1:keep] = vn
        mm = jnp.maximum(jnp.maximum(ms[0], ms[1]), ms[2])
        w = [jnp.exp2(m - mm) for m in ms]
        den = w[0] * ss[0] + w[1] * ss[1] + w[2] * ss[2]
        num = w[0] * nums[0] + w[1] * nums[1] + w[2] * nums[2]
        out = jnp.where(out_lane == hh, num / den, out)
    o_ref[0] = out


N_SAMPLE_IN = 9


def _ffn_body(*refs, tm, nt, has_pre, n_sample_in, n_alias):
    if has_pre:
        x_ref, a_ref, wa_ref = refs[:3]
        refs = refs[3:]
    else:
        x_ref = refs[0]
        refs = refs[1:]
    g_ref, wg_ref, wu_ref, cw_ref, cb_ref, wd_ref = refs[:6]
    sample_in = refs[6:6 + n_sample_in]
    outs = refs[6 + n_sample_in + n_alias:-1]
    o_ref, hist_ref = outs[0], outs[1]
    if n_sample_in:
        sample_out = outs[2:]
    else:
        for z_ref in outs[2:]:
            z_ref[...] = jnp.zeros(z_ref.shape, z_ref.dtype)
    gbuf = refs[-1]
    t = pl.program_id(1)

    @pl.when(t == 0)
    def _():
        gbuf[0:SUBLANES, :] = jnp.zeros((SUBLANES, D_FF), F32)

    if n_sample_in:
        q_ref, kvn_ref = sample_in[0:2]
        heads = sample_out[0].shape[-1]
        step = pl.program_id(0) * nt + t
        _sample_heads((step % (N_HEADS // heads)) * heads, q_ref, kvn_ref, sample_in[2:5], sample_in[5:8],
                      sample_in[8], sample_out[0], sample_out[1:4])
    x = x_ref[0]
    if has_pre:
        x = x + _dot(a_ref[0].astype(BF), wa_ref[...])
    h = _rms(x, g_ref[...]).astype(BF)
    gbuf[SUBLANES:SUBLANES + tm, :] = _dot(h, wg_ref[...])
    up = _dot(h, wu_ref[...])
    z = _conv3(gbuf, cw_ref, tm) + cb_ref[...]
    act = _silu(z) * up
    o_ref[0] = x + _dot(act.astype(BF), wd_ref[...])
    _carry_rows(gbuf, hist_ref, tm)


def _ffn_prompt(x, pre, layer, g, w_gate, w_up, conv_w, conv_b, w_down, sample=None, zero_fill=()):
    b, t, d = x.shape
    tm = FFN_TILE if sample is None and not zero_fill else FFN_TILE_WITH_SAMPLE
    nt = t // tm
    row = lambda i, j: (i, j, 0)
    in_specs = [pl.BlockSpec((1, tm, d), row)]
    args = [x]
    if pre is not None:
        a, wa, pre_layer = pre
        in_specs += [pl.BlockSpec((1, tm, a.shape[-1]), row), _layer_spec(wa, pre_layer)]
        args += [a, wa]
    weights = (g, w_gate, w_up, conv_w, conv_b, w_down)
    in_specs += [_layer_spec(a, layer) for a in weights]
    args += weights
    out_specs = [
        pl.BlockSpec((1, tm, d), row),
        pl.BlockSpec((1, 2, D_FF), lambda i, j: (i, 0, 0)),
    ]
    out_shape = [
        jax.ShapeDtypeStruct((b, t, d), F32),
        jax.ShapeDtypeStruct((b, 2, D_FF), F32),
    ]
    aliases = {}
    if sample is not None:
        q, kvn, caches_t, attn_layer, bts, b0, prev = sample
        nb = q.shape[0]
        steps = b * nt
        parts = steps // nb
        heads = N_HEADS // parts
        assert parts * nb == steps and heads * parts == N_HEADS
        cache_specs = []
        for c, (win, dil) in zip(caches_t, GROUPS):
            assert c.shape[-1] == win, "the cache must hold one full window"
            cache_specs.append(pl.BlockSpec(
                (1, 1, 2, heads, HEAD_DIM, win),
                lambda i, j: (attn_layer, (i * nt + j) // parts, 0, (i * nt + j) % parts, 0, 0)))
        in_specs += [
            pl.BlockSpec((1,) + q.shape[1:], lambda i, j: ((i * nt + j) // parts, 0, 0, 0)),
            pl.BlockSpec((1,) + kvn.shape[1:], lambda i, j: ((i * nt + j) // parts, 0, 0, 0, 0)),
            *cache_specs,
            *[_const_spec(bt.shape) for bt in bts],
            _const_spec(b0.shape),
        ]
        args += [q, kvn, *caches_t, *bts, b0]
        if prev is not None:
            in_specs += [pl.BlockSpec(memory_space=pl.ANY)] * N_GROUPS
            aliases = {len(args) + k: 3 + k for k in range(N_GROUPS)}
            args += list(prev)
        out_specs += [pl.BlockSpec((1, HEAD_DIM, heads), lambda i, j: (i * nt + j, 0, 0)), *cache_specs]
        out_shape += [jax.ShapeDtypeStruct((steps, HEAD_DIM, heads), F32)]
        out_shape += [jax.ShapeDtypeStruct(c.shape, c.dtype) for c in caches_t]
    zero_specs, zero_shapes = _zero_fill_outputs(zero_fill, nt, b * nt)
    out_specs += zero_specs
    out_shape += zero_shapes
    return pl.pallas_call(
        functools.partial(_ffn_body, tm=tm, nt=nt, has_pre=pre is not None,
                          n_sample_in=0 if sample is None else N_SAMPLE_IN, n_alias=len(aliases)),
        grid=(b, nt),
        in_specs=in_specs,
        out_specs=out_specs,
        out_shape=out_shape,
        input_output_aliases=aliases,
        scratch_shapes=[pltpu.VMEM((tm + SUBLANES, D_FF), F32)],
        compiler_params=_cparams(("arbitrary", "arbitrary")),
        name="ffn_prompt",
    )(*args)


def _qkv_body(x_ref, g_ref, w_ref, qg_ref, kg_ref, bd_ref, *refs, tm, n_alias):
    refs = refs[n_alias:]
    qkv_refs = refs[:3 * N_GROUPS]
    cache_refs = refs[3 * N_GROUPS:4 * N_GROUPS]
    h = _rms(x_ref[0], g_ref[...]).astype(BF)
    bd = bd_ref[...]
    for g, (win, dil) in reversed(list(enumerate(GROUPS))):
        c0 = g * 3 * ATTN_WIDTH
        q = _dot(h, w_ref[:, c0:c0 + ATTN_WIDTH])
        k = _dot(h, w_ref[:, c0 + ATTN_WIDTH:c0 + 2 * ATTN_WIDTH])
        v = _dot(h, w_ref[:, c0 + 2 * ATTN_WIDTH:c0 + 3 * ATTN_WIDTH])
        q = _head_rms(q, qg_ref[...], bd)
        k = _head_rms(k, kg_ref[...], bd)

        c_ref = cache_refs[g]
        rows = min(win, tm)
        c_ref[0, :, 0:ATTN_WIDTH] = k[tm - rows:, :]
        c_ref[0, :, ATTN_WIDTH:2 * ATTN_WIDTH] = v[tm - rows:, :]

        for o_ref, val in zip(qkv_refs[3 * g:3 * g + 3], (q, k, v)):
            if dil == 1:
                o_ref[0, 0, 0] = val.astype(BF)
            else:
                by_residue = pltpu.einshape("abc->bac", val.reshape(tm // dil, dil, ATTN_WIDTH))
                o_ref[0, 0] = by_residue.astype(BF)


def _qkv_prompt(x, layer, attn_layer, g, w_qkv, q_gain, k_gain, bd, prev_rows, tm=QKV_TILE):
    n_attn = w_qkv.shape[0]
    b, t, d = x.shape
    nt = t // tm
    tiles_per_super = SUPER // tm
    out_specs, out_shapes = [], []
    for win, dil in GROUPS:
        rows = tm // dil
        shp = (b, t // SUPER, dil, SUPER // dil, ATTN_WIDTH)
        spec = pl.BlockSpec(
            (1, 1, dil, rows, ATTN_WIDTH),
            lambda i, j: (i, j // tiles_per_super, 0, j % tiles_per_super, 0))
        out_specs += [spec] * 3
        out_shapes += [jax.ShapeDtypeStruct(shp, BF)] * 3
    for win, dil in GROUPS:
        keep = win
        if keep >= tm:
            first = nt - keep // tm
            spec = pl.BlockSpec((None, 1, tm, 2 * ATTN_WIDTH),
                                lambda i, j, first=first: (attn_layer, i, jnp.maximum(j - first, 0), 0))
        else:
            spec = pl.BlockSpec((None, 1, keep, 2 * ATTN_WIDTH), lambda i, j: (attn_layer, i, 0, 0))
        out_specs.append(spec)
        out_shapes.append(jax.ShapeDtypeStruct((n_attn, b, keep, 2 * ATTN_WIDTH), F32))
    n_in = 6
    aliases = {} if prev_rows is None else {n_in + k: 3 * N_GROUPS + k for k in range(N_GROUPS)}
    res = pl.pallas_call(
        functools.partial(_qkv_body, tm=tm, n_alias=len(aliases)),
        grid=(b, nt),
        in_specs=[
            pl.BlockSpec((1, tm, d), lambda i, j: (i, j, 0)),
            _layer_spec(g, layer),
            *[_layer_spec(a, attn_layer) for a in (w_qkv, q_gain, k_gain)],
            _const_spec((ATTN_WIDTH, ATTN_WIDTH)),
            *[pl.BlockSpec(memory_space=pl.ANY)] * len(aliases),
        ],
        out_specs=out_specs,
        out_shape=out_shapes,
        input_output_aliases=aliases,
        compiler_params=_cparams(("arbitrary", "arbitrary")),
        name="qkv_prompt",
    )(x, g, w_qkv, q_gain, k_gain, bd, *(prev_rows or ()))
    qkv = [r.reshape(b, t, ATTN_WIDTH) for r in res[:3 * N_GROUPS]]
    return qkv, res[3 * N_GROUPS:]


def _attn_body(*refs):
    qkv_refs = refs[:3 * N_GROUPS]
    tb_ref = refs[3 * N_GROUPS]
    o_ref = refs[3 * N_GROUPS + 1]
    out_s, lse_s = refs[3 * N_GROUPS + 2:]
    n_super = o_ref.shape[1] // SUPER
    blocks = SUPER // QBLK
    lane = lax.broadcasted_iota(jnp.int32, (QBLK, LANES), 1)
    first_head = lane < HEAD_DIM
    contract_last = (((1,), (1,)), ((), ()))
    zero = jnp.zeros((), BF)
    lane2 = lax.broadcasted_iota(jnp.int32, (2 * QBLK, LANES), 1)
    first_head2 = lane2 < HEAD_DIM
    ones_h0 = jnp.where(first_head2, 1.0, 0.0).astype(BF)
    ones_h1 = jnp.where(first_head2, 0.0, 1.0).astype(BF)

    for s in range(n_super):
        for g, (win, dil) in enumerate(GROUPS):
            q_ref, k_ref, v_ref = qkv_refs[3 * g:3 * g + 3]
            per_res = blocks // dil

            def trip(it, carry, s=s, g=g, dil=dil, per_res=per_res,
                     q_ref=q_ref, k_ref=k_ref, v_ref=v_ref):
                def logits_of(u):
                    n = it * ATTN_UNROLL + u
                    c = n % per_res
                    r = n // per_res
                    start = s * SUPER + n * QBLK
                    first = int(c == 0 and s == 0)
                    if c > 0:
                        prev = start - QBLK
                    elif s == 0:
                        prev = start
                    else:
                        prev = start - SUPER + SUPER // dil - QBLK
                    qb = q_ref[0, pl.ds(start, QBLK), :]
                    q2 = jnp.concatenate(
                        [jnp.where(first_head, qb, zero), jnp.where(first_head, zero, qb)], axis=0)
                    k2 = jnp.concatenate(
                        [k_ref[0, pl.ds(prev, QBLK), :], k_ref[0, pl.ds(start, QBLK), :]], axis=0)
                    logits = lax.dot_general(q2, k2, contract_last, preferred_element_type=F32)
                    return logits + tb_ref[g, first, 0], start, prev, r + dil * QBLK * c

                def probs_of(logits, start, prev, nat):
                    m = jnp.max(logits, axis=1, keepdims=True)
                    return jnp.exp2(logits - m).astype(BF), m, start, prev, nat

                def finish(e, m, start, prev, nat):
                    v2 = jnp.concatenate(
                        [v_ref[0, pl.ds(prev, QBLK), :], v_ref[0, pl.ds(start, QBLK), :]], axis=0)
                    rhs = jnp.concatenate([
                        jnp.concatenate([jnp.where(first_head2, v2, zero), ones_h0], axis=1),
                        jnp.concatenate([jnp.where(first_head2, zero, v2), ones_h1], axis=1)], axis=0)
                    pv = _dot(jnp.concatenate([e[0:QBLK], e[QBLK:]], axis=1), rhs)
                    rows = pl.ds(nat, QBLK) if dil == 1 else pl.ds(nat, QBLK, stride=dil)
                    row_sum = pv[:, LANES:]
                    out_s[s, g, rows, :] = pv[:, 0:LANES] / row_sum
                    lse_s[s, g, rows, :] = jnp.where(first_head, m[0:QBLK], m[QBLK:]) + jnp.log2(row_sum)

                staged, probs = {}, {}
                for u in range(ATTN_UNROLL + 2):
                    if u < ATTN_UNROLL:
                        staged[u] = logits_of(u)
                    if 1 <= u <= ATTN_UNROLL:
                        probs[u - 1] = probs_of(*staged.pop(u - 1))
                    if u >= 2:
                        finish(*probs.pop(u - 2))
                return carry

            for it in range(blocks // ATTN_UNROLL):
                trip(it, 0)

        for i in range(SUPER // MERGE_ROWS):
            rows = slice(i * MERGE_ROWS, (i + 1) * MERGE_ROWS)
            l0, l1, l2 = lse_s[s, 0, rows, :], lse_s[s, 1, rows, :], lse_s[s, 2, rows, :]
            top = jnp.maximum(jnp.maximum(l0, l1), l2)
            w0, w1, w2 = jnp.exp2(l0 - top), jnp.exp2(l1 - top), jnp.exp2(l2 - top)
            num = w0 * out_s[s, 0, rows, :] + w1 * out_s[s, 1, rows, :] + w2 * out_s[s, 2, rows, :]
            o_ref[0, s * SUPER + i * MERGE_ROWS:s * SUPER + (i + 1) * MERGE_ROWS, :] = num / (w0 + w1 + w2)


def _attn_prompt(qkv, tables):
    b, t, _ = qkv[0].shape
    pairs = N_HEADS // 2
    col = lambda i, j: (i, 0, j)
    return pl.pallas_call(
        _attn_body,
        grid=(b, pairs),
        in_specs=[pl.BlockSpec((1, t, LANES), col)] * (3 * N_GROUPS) + [
            pl.BlockSpec((N_GROUPS, 2, 1, 2 * QBLK, 2 * QBLK), lambda i, j: (0, 0, j, 0, 0)),
        ],
        out_specs=pl.BlockSpec((1, t, LANES), col),
        out_shape=jax.ShapeDtypeStruct((b, t, ATTN_WIDTH), F32),
        scratch_shapes=[pltpu.VMEM((t // SUPER, N_GROUPS, SUPER, LANES), F32)] * 2,
        compiler_params=_cparams(("arbitrary", "arbitrary")),
        name="attn_prompt",
    )(*qkv, tables)


def _mixer_sample_body(x_ref, g_ref, win_ref, cw_ref, wout_ref, h0_ref, h1_ref, o_ref, u_ref):
    x = x_ref[...]
    h = _rms(x, g_ref[...]).astype(BF)
    bg = _dot(h, win_ref[:, 0:D_MODEL])
    cg = _dot(h, win_ref[:, D_MODEL:2 * D_MODEL])
    xv = _dot(h, win_ref[:, 2 * D_MODEL:3 * D_MODEL])
    u = cg * xv
    conv = cw_ref[0:1, :] * h0_ref[...]
    conv = conv + cw_ref[1:2, :] * h1_ref[...]
    conv = conv + cw_ref[2:3, :] * u
    o_ref[...] = x + _dot((bg * conv).astype(BF), wout_ref[...])
    u_ref[...] = u


def _mixer_sample(x, layer, mix_layer, g, w_in, conv_w, w_out, h0, h1):
    n, d = x.shape
    return pl.pallas_call(
        _mixer_sample_body,
        grid=(1,),
        in_specs=[_const_spec(x.shape), _layer_spec(g, layer),
                  *[_layer_spec(a, mix_layer) for a in (w_in, conv_w, w_out)],
                  _const_spec(h0.shape), _const_spec(h1.shape)],
        out_specs=[_whole_spec((n, d))] * 2,
        out_shape=[jax.ShapeDtypeStruct((n, d), F32)] * 2,
        compiler_params=_cparams(("arbitrary",)),
        name="mixer_sample",
    )(x, g, w_in, conv_w, w_out, h0, h1)


def _ffn_sample_body(*refs, has_pre):
    if has_pre:
        x_ref, a_ref, wa_ref = refs[:3]
        refs = refs[3:]
    else:
        x_ref = refs[0]
        refs = refs[1:]
    g_ref, wg_ref, wu_ref, cw_ref, cb_ref, wd_ref, h0_ref, h1_ref, o_ref, gate_ref = refs
    x = x_ref[...]
    if has_pre:
        x = x + _dot(a_ref[...].astype(BF), wa_ref[...])
    h = _rms(x, g_ref[...]).astype(BF)
    gate = _dot(h, wg_ref[...])
    up = _dot(h, wu_ref[...])
    z = cw_ref[0:1, :] * h0_ref[...]
    z = z + cw_ref[1:2, :] * h1_ref[...]
    z = z + cw_ref[2:3, :] * gate
    z = z + cb_ref[...]
    act = _silu(z) * up
    o_ref[...] = x + _dot(act.astype(BF), wd_ref[...])
    gate_ref[...] = gate


def _ffn_sample(x, pre, layer, g, w_gate, w_up, conv_w, conv_b, w_down, h0, h1):
    n, d = x.shape
    args, in_specs = [x], [_const_spec(x.shape)]
    if pre is not None:
        a, wa, pre_layer = pre
        args += [a, wa]
        in_specs += [_const_spec(a.shape), _layer_spec(wa, pre_layer)]
    weights = (g, w_gate, w_up, conv_w, conv_b, w_down)
    args += [*weights, h0, h1]
    in_specs += [*[_layer_spec(a, layer) for a in weights], _const_spec(h0.shape), _const_spec(h1.shape)]
    return pl.pallas_call(
        functools.partial(_ffn_sample_body, has_pre=pre is not None),
        grid=(1,),
        in_specs=in_specs,
        out_specs=[_whole_spec((n, d)), _whole_spec((n, D_FF))],
        out_shape=[jax.ShapeDtypeStruct((n, d), F32), jax.ShapeDtypeStruct((n, D_FF), F32)],
        compiler_params=_cparams(("arbitrary",)),
        name="ffn_sample",
    )(*args)


def _qkv_sample_body(x_ref, g_ref, w_ref, qg_ref, kg_ref, bd_ref, q_ref, kv_ref):
    h = _rms(x_ref[...], g_ref[...]).astype(BF)
    bd = bd_ref[...]
    for g in range(N_GROUPS):
        c0 = g * 3 * ATTN_WIDTH
        q = _dot(h, w_ref[:, c0:c0 + ATTN_WIDTH])
        k = _dot(h, w_ref[:, c0 + ATTN_WIDTH:c0 + 2 * ATTN_WIDTH])
        v = _dot(h, w_ref[:, c0 + 2 * ATTN_WIDTH:c0 + 3 * ATTN_WIDTH])
        q_ref[:, g * ATTN_WIDTH:(g + 1) * ATTN_WIDTH] = _head_rms(q, qg_ref[...], bd)
        kv_ref[:, 2 * g * ATTN_WIDTH:(2 * g + 1) * ATTN_WIDTH] = _head_rms(k, kg_ref[...], bd)
        kv_ref[:, (2 * g + 1) * ATTN_WIDTH:(2 * g + 2) * ATTN_WIDTH] = v


def _qkv_sample(x, layer, attn_layer, g, w_qkv, q_gain, k_gain, bd):
    n = x.shape[0]
    out_cols = (N_GROUPS * ATTN_WIDTH, N_GROUPS * 2 * ATTN_WIDTH)
    return pl.pallas_call(
        _qkv_sample_body,
        grid=(1,),
        in_specs=[_const_spec(x.shape), _layer_spec(g, layer),
                  *[_layer_spec(a, attn_layer) for a in (w_qkv, q_gain, k_gain)], _const_spec(bd.shape)],
        out_specs=[_whole_spec((n, c)) for c in out_cols],
        out_shape=[jax.ShapeDtypeStruct((n, c), F32) for c in out_cols],
        compiler_params=_cparams(("arbitrary",)),
        name="qkv_sample",
    )(x, g, w_qkv, q_gain, k_gain, bd)


def _t5_bucket(dist):
    exact = N_BUCKETS // 2
    n = np.asarray(dist, dtype=np.float32)
    large = exact + np.log(np.maximum(n, 1.0) / exact) / math.log(MAX_DISTANCE / exact) * (N_BUCKETS - exact)
    large = np.minimum(np.floor(large), N_BUCKETS - 1)
    return np.where(n < exact, n, large).astype(np.int32)


def _bias_tables(rel_bias):
    ci = np.arange(2 * QBLK)[None, None, :]
    period = 3 * QBLK
    tabs, bts, b0s = [], [], []
    for g, (win, dil) in enumerate(GROUPS):
        bucket = _t5_bucket(dil * np.arange(N_KEYS + 1))
        bias = jnp.take(rel_bias, bucket, axis=0)[:, g * N_HEADS:(g + 1) * N_HEADS].T
        bias = bias * LOG2E
        w = jnp.concatenate([bias[:, ::-1], jnp.full((N_HEADS, period - N_KEYS - 1), NEG, F32)], axis=1)
        band = jnp.tile(w, (1, QBLK))[:, :QBLK * (period - 1)]
        band = band.reshape(N_HEADS, QBLK, period - 1)[:, :, :2 * QBLK]
        band_first = jnp.where(ci >= QBLK, band, NEG)
        pair_rows = (N_HEADS // 2, 2 * QBLK, 2 * QBLK)
        tabs.append(jnp.stack([band.reshape(pair_rows), band_first.reshape(pair_rows)], axis=0))
        vals = bias[:, N_KEYS:0:-1][:, :, None]
        skipped = jnp.full((N_HEADS, N_KEYS, dil - 1), NEG, F32)
        bts.append(jnp.concatenate([vals, skipped], axis=2).reshape(N_HEADS, 1, win))
        b0s.append(bias[:, 0:1][:, :, None])
    return jnp.stack(tabs), bts, jnp.stack(b0s)


def kernel(x_prompt, x_sample, state_sc_conv, cache_kv_d1, cache_kv_d4, cache_kv_d16, state_ffn_conv,
           norm_mix, norm_ffn, sc_w_in, sc_conv_w, sc_w_out, attn_w_qkv, attn_q_norm, attn_k_norm,
           attn_w_out, rel_bias, ffn_w_gate, ffn_w_up, ffn_conv_w, ffn_conv_b, ffn_w_down):
    depth = norm_mix.shape[0]
    nb = x_sample.shape[0]
    caches_t = [jnp.transpose(c, (0, 1, 3, 4, 5, 2)) for c in (cache_kv_d1, cache_kv_d4, cache_kv_d16)]
    tables, bts, b0 = _bias_tables(rel_bias)
    seg = np.arange(ATTN_WIDTH) // HEAD_DIM
    bd = jnp.asarray(seg[:, None] == seg[None, :], BF)

    mixer_w = (sc_w_in.astype(BF), sc_conv_w, sc_w_out.astype(BF))
    w_qkv_b, w_ao_b = attn_w_qkv.astype(BF), attn_w_out.astype(BF)
    ffn_w = (norm_ffn[:, None, :], ffn_w_gate.astype(BF), ffn_w_up.astype(BF), ffn_conv_w,
             ffn_conv_b[:, None, :], ffn_w_down.astype(BF))
    g_mix = norm_mix[:, None, :]
    q_gain = (jnp.tile(attn_q_norm, (1, N_HEADS)) * (SCALE * LOG2E))[:, None, :]
    k_gain = jnp.tile(attn_k_norm, (1, N_HEADS))[:, None, :]

    xp = x_prompt
    xs = x_sample[:, 0, :]
    p_sc, s_sc, p_ffn, s_ffn = [], [], [], []
    kv_rows = new_caches = None
    for i in range(depth):
        j = i // 2
        sf = state_ffn_conv[i]
        if i % 2 == 0:
            xp, hist = _mixer_prompt(xp, i, j, g_mix, *mixer_w)
            p_sc.append(hist)
            st = state_sc_conv[j]
            xs, u = _mixer_sample(xs, i, j, g_mix, *mixer_w, st[:, 0], st[:, 1])
            s_sc.append(jnp.stack([st[:, 1], u], axis=1))
            pre_p = pre_s = None
        else:
            qkv, kv_rows = _qkv_prompt(xp, i, j, g_mix, w_qkv_b, q_gain, k_gain, bd, kv_rows)
            pre_p = (_attn_prompt(qkv, tables), w_ao_b, j)

            qs, kvs = _qkv_sample(xs, i, j, g_mix, w_qkv_b, q_gain, k_gain, bd)
            qs = jnp.swapaxes(qs.reshape(nb, N_GROUPS, N_HEADS, HEAD_DIM), -1, -2)
            kvs = jnp.swapaxes(kvs.reshape(nb, N_GROUPS, 2, N_HEADS, HEAD_DIM), -1, -2)
            xp, fh, a_s, *new_caches = _ffn_prompt(
                xp, pre_p, i, *ffn_w, sample=(qs, kvs, caches_t, j, bts, b0, new_caches))
            a_s = jnp.swapaxes(a_s.reshape(nb, -1, HEAD_DIM, a_s.shape[-1]), -1, -2)
            pre_s = (a_s.reshape(nb, ATTN_WIDTH), w_ao_b, j)
        if i == 0:
            n_attn = attn_w_qkv.shape[0]
            fills = [(c.shape, (1, 1) + c.shape[2:]) for c in caches_t]
            fills += [((n_attn, xp.shape[0], win, 2 * ATTN_WIDTH), (1, 1, win // SUBLANES, 2 * ATTN_WIDTH))
                      for win, _ in GROUPS]
            xp, fh, *zeros = _ffn_prompt(xp, pre_p, i, *ffn_w, zero_fill=fills)
            new_caches, kv_rows = zeros[:N_GROUPS], zeros[N_GROUPS:]
        elif i % 2 == 0:
            xp, fh = _ffn_prompt(xp, pre_p, i, *ffn_w)
        p_ffn.append(fh)
        xs, gate = _ffn_sample(xs, pre_s, i, *ffn_w, sf[:, 0], sf[:, 1])
        s_ffn.append(jnp.stack([sf[:, 1], gate], axis=1))

    s_kv = [jnp.transpose(c, (0, 1, 5, 2, 3, 4)) for c in new_caches]
    p_kv = [r.reshape(r.shape[:3] + (2, N_HEADS, HEAD_DIM)) for r in kv_rows]
    return (xp, xs[:, None, :],
            jnp.stack(p_sc, axis=0), p_kv[0], p_kv[1], p_kv[2], jnp.stack(p_ffn, axis=0),
            jnp.stack(s_sc, axis=0), s_kv[0], s_kv[1], s_kv[2], jnp.stack(s_ffn, axis=0))
```

```python
import functools
import math

import numpy as np
import jax
import jax.numpy as jnp
from jax import lax
from jax.experimental import pallas as pl
from jax.experimental.pallas import tpu as pltpu

D_MODEL = 1024
D_FF = 2816
N_HEADS = 8
HEAD_DIM = 64
ATTN_WIDTH = N_HEADS * HEAD_DIM
GROUPS = ((128, 1), (512, 4), (2048, 16))
N_GROUPS = len(GROUPS)
N_KEYS = 128
N_BUCKETS = 32
MAX_DISTANCE = 2048
EPS = 1e-6
NEG = -1e30
SCALE = HEAD_DIM ** -0.5
LOG2E = math.log2(math.e)
SUPER = 2048
QBLK = 128
LANES = 128
SUBLANES = 8
ATTN_UNROLL = 16
MERGE_ROWS = 256
MIXER_TILE = 1024
QKV_TILE = 512
FFN_TILE = 512
FFN_TILE_WITH_SAMPLE = 256
BF = jnp.bfloat16
F32 = jnp.float32

V7X_VMEM_BYTES = 64 * 1024 * 1024
VMEM_LIMIT = V7X_VMEM_BYTES - 8 * 1024 * 1024


def _cparams(sem):
    return pltpu.CompilerParams(dimension_semantics=sem, vmem_limit_bytes=VMEM_LIMIT)


def _const_spec(shape):
    nd = len(shape)
    return pl.BlockSpec(shape, lambda *_: (0,) * nd, pipeline_mode=pl.Buffered(1))


def _whole_spec(shape):
    nd = len(shape)
    return pl.BlockSpec(shape, lambda *_: (0,) * nd)


def _layer_spec(arr, layer):
    nd = arr.ndim
    return pl.BlockSpec((None,) + arr.shape[1:], lambda *_: (layer,) + (0,) * (nd - 1),
                        pipeline_mode=pl.Buffered(1))


def _dot(a, b):
    return jnp.dot(a, b, preferred_element_type=F32)


def _rms(x, g):
    r = lax.rsqrt(jnp.mean(x * x, axis=-1, keepdims=True) + EPS)
    return (x * r) * g


def _silu(z):
    half = 0.5 * z
    return half + half * jnp.tanh(half)


def _head_rms(x, gain, bd):
    ss = _dot((x * x).astype(BF), bd)
    r = lax.rsqrt(ss * (1.0 / HEAD_DIM) + EPS)
    return (x * r) * gain


def _conv3(buf, cw_ref, tm):
    ext = buf[0:tm + SUBLANES, :]
    back2 = pltpu.roll(ext, 2, axis=0)[SUBLANES:, :]
    back1 = pltpu.roll(ext, 1, axis=0)[SUBLANES:, :]
    out = cw_ref[0:1, :] * back2
    out = out + cw_ref[1:2, :] * back1
    return out + cw_ref[2:3, :] * ext[SUBLANES:, :]


def _carry_rows(buf, hist_ref, tm):
    tail = buf[tm:tm + SUBLANES, :]
    buf[0:SUBLANES, :] = tail
    hist_ref[0] = tail[SUBLANES - 2:, :]


def _mixer_body(x_ref, g_ref, win_ref, cw_ref, wout_ref, o_ref, hist_ref, ubuf, *, tm):
    t = pl.program_id(1)

    @pl.when(t == 0)
    def _():
        ubuf[0:SUBLANES, :] = jnp.zeros((SUBLANES, D_MODEL), F32)

    x = x_ref[0]
    h = _rms(x, g_ref[...]).astype(BF)
    bg = _dot(h, win_ref[:, 0:D_MODEL])
    cg = _dot(h, win_ref[:, D_MODEL:2 * D_MODEL])
    xv = _dot(h, win_ref[:, 2 * D_MODEL:3 * D_MODEL])
    ubuf[SUBLANES:SUBLANES + tm, :] = cg * xv
    conv = _conv3(ubuf, cw_ref, tm)
    y = _dot((bg * conv).astype(BF), wout_ref[...])
    o_ref[0] = x + y
    _carry_rows(ubuf, hist_ref, tm)


def _mixer_prompt(x, layer, mix_layer, g, w_in, conv_w, w_out, tm=MIXER_TILE):
    b, t, d = x.shape
    nt = t // tm
    return pl.pallas_call(
        functools.partial(_mixer_body, tm=tm),
        grid=(b, nt),
        in_specs=[
            pl.BlockSpec((1, tm, d), lambda i, j: (i, j, 0)),
            _layer_spec(g, layer),
            *[_layer_spec(a, mix_layer) for a in (w_in, conv_w, w_out)],
        ],
        out_specs=[
            pl.BlockSpec((1, tm, d), lambda i, j: (i, j, 0)),
            pl.BlockSpec((1, 2, d), lambda i, j: (i, 0, 0)),
        ],
        out_shape=[
            jax.ShapeDtypeStruct((b, t, d), F32),
            jax.ShapeDtypeStruct((b, 2, d), F32),
        ],
        scratch_shapes=[pltpu.VMEM((tm + SUBLANES, d), F32)],
        compiler_params=_cparams(("arbitrary", "arbitrary")),
        name="mixer_prompt",
    )(x, g, w_in, conv_w, w_out)


def _sample_heads(first_head, q_ref, kvn_ref, c_refs, bt_refs, b0_ref, o_ref, n_refs):
    heads = o_ref.shape[-1]
    head_lane = lax.broadcasted_iota(jnp.int32, (HEAD_DIM, N_HEADS), 1)
    out_lane = lax.broadcasted_iota(jnp.int32, (HEAD_DIM, heads), 1)
    out = jnp.zeros((HEAD_DIM, heads), F32)
    for hh in range(heads):
        h = first_head + hh
        mine = head_lane == h
        column = lambda x: jnp.sum(jnp.where(mine, x, 0.0), axis=1, keepdims=True)
        ms, ss, nums = [], [], []
        for g in range(N_GROUPS):
            c_ref, n_ref = c_refs[g], n_refs[g]
            keep = c_ref.shape[-1]
            qc = column(q_ref[0, g])
            kn = column(kvn_ref[0, g, 0])
            vn = column(kvn_ref[0, g, 1])
            kt = c_ref[0, 0, 0, hh]
            vt = c_ref[0, 0, 1, hh]
            l = jnp.sum(kt * qc, axis=0, keepdims=True) + bt_refs[g][h]
            ln = jnp.sum(kn * qc, axis=0, keepdims=True) + b0_ref[g, h]
            m = jnp.maximum(jnp.max(l, axis=1, keepdims=True), ln)
            e = jnp.exp2(l - m)
            en = jnp.exp2(ln - m)
            ss.append(jnp.sum(e, axis=1, keepdims=True) + en)
            nums.append(jnp.sum(vt * e, axis=1, keepdims=True) + vn * en)
            ms.append(m)
            n_ref[0, 0, 0, hh] = pltpu.roll(kt, keep - 1, axis=1)
            n_ref[0, 0, 1, hh] = pltpu.roll(vt, keep - 1, axis=1)
            n_ref[0, 0, 0, hh, :, keep - 1:keep] = kn
            n_ref[0, 0, 1, hh, :, keep - 1:keep] = vn
        mm = jnp.maximum(jnp.maximum(ms[0], ms[1]), ms[2])
        w = [jnp.exp2(m - mm) for m in ms]
        den = w[0] * ss[0] + w[1] * ss[1] + w[2] * ss[2]
        num = w[0] * nums[0] + w[1] * nums[1] + w[2] * nums[2]
        out = jnp.where(out_lane == hh, num / den, out)
    o_ref[0] = out


N_SAMPLE_IN = 9


def _ffn_body(*refs, tm, nt, steps, has_pre, n_sample_in, n_alias, n_zero):
    if has_pre:
        x_ref, a_ref, wa_ref = refs[:3]
        refs = refs[3:]
    else:
        x_ref = refs[0]
        refs = refs[1:]
    g_ref, wg_ref, wu_ref, cw_ref, cb_ref, wd_ref = refs[:6]
    sample_in = refs[6:6 + n_sample_in]
    n_scratch = 4 if n_zero else 1
    outs = refs[6 + n_sample_in + n_alias:-n_scratch]
    o_ref, hist_ref, sample_out = outs[0], outs[1], outs[2:]
    gbuf = refs[-n_scratch]
    t = pl.program_id(1)
    step = pl.program_id(0) * nt + t

    @pl.when(t == 0)
    def _():
        gbuf[0:SUBLANES, :] = jnp.zeros((SUBLANES, D_FF), F32)

    zero_copies = []
    if n_zero:
        zc, zr, zsem = refs[-3:]

        @pl.when(step == 0)
        def _():
            zc[...] = jnp.zeros(zc.shape, F32)
            zr[...] = jnp.zeros(zr.shape, F32)

        for z_ref in outs[2:2 + n_zero]:
            if len(z_ref.shape) == 6:
                n_l, n_b, keep = z_ref.shape[0], z_ref.shape[1], z_ref.shape[-1]
                for k in range(n_l * n_b // steps):
                    blk = step * (n_l * n_b // steps) + k
                    zero_copies.append(pltpu.make_async_copy(
                        zc.at[:, :, :, pl.ds(0, keep)], z_ref.at[blk // n_b, blk % n_b],
                        zsem.at[len(zero_copies)]))
            else:
                n_l, n_b, win = z_ref.shape[0], z_ref.shape[1], z_ref.shape[2]
                per_row = steps // (n_l * n_b)
                rows = win // per_row
                zero_copies.append(pltpu.make_async_copy(
                    zr.at[pl.ds(0, rows), :],
                    z_ref.at[step // (n_b * per_row), (step // per_row) % n_b,
                             pl.ds((step % per_row) * rows, rows), :],
                    zsem.at[len(zero_copies)]))
        for cp in zero_copies:
            cp.start()

    if n_sample_in:
        q_ref, kvn_ref = sample_in[0:2]
        heads = sample_out[0].shape[-1]
        _sample_heads((step % (N_HEADS // heads)) * heads, q_ref, kvn_ref, sample_in[2:5], sample_in[5:8],
                      sample_in[8], sample_out[0], sample_out[1:4])
    x = x_ref[0]
    if has_pre:
        x = x + _dot(a_ref[0].astype(BF), wa_ref[...])
    h = _rms(x, g_ref[...]).astype(BF)
    gbuf[SUBLANES:SUBLANES + tm, :] = _dot(h, wg_ref[...])
    up = _dot(h, wu_ref[...])
    z = _conv3(gbuf, cw_ref, tm) + cb_ref[...]
    act = _silu(z) * up
    o_ref[0] = x + _dot(act.astype(BF), wd_ref[...])
    _carry_rows(gbuf, hist_ref, tm)
    for cp in zero_copies:
        cp.wait()


def _ffn_prompt(x, pre, layer, g, w_gate, w_up, conv_w, conv_b, w_down, sample=None, zero_fill=()):
    b, t, d = x.shape
    tm = FFN_TILE if sample is None else FFN_TILE_WITH_SAMPLE
    nt = t // tm
    row = lambda i, j: (i, j, 0)
    in_specs = [pl.BlockSpec((1, tm, d), row)]
    args = [x]
    if pre is not None:
        a, wa, pre_layer = pre
        in_specs += [pl.BlockSpec((1, tm, a.shape[-1]), row), _layer_spec(wa, pre_layer)]
        args += [a, wa]
    weights = (g, w_gate, w_up, conv_w, conv_b, w_down)
    in_specs += [_layer_spec(a, layer) for a in weights]
    args += weights
    out_specs = [
        pl.BlockSpec((1, tm, d), row),
        pl.BlockSpec((1, 2, D_FF), lambda i, j: (i, 0, 0)),
    ]
    out_shape = [
        jax.ShapeDtypeStruct((b, t, d), F32),
        jax.ShapeDtypeStruct((b, 2, D_FF), F32),
    ]
    aliases = {}
    if sample is not None:
        q, kvn, caches_t, attn_layer, bts, b0, prev = sample
        nb = q.shape[0]
        steps = b * nt
        parts = steps // nb
        heads = N_HEADS // parts
        assert parts * nb == steps and heads * parts == N_HEADS
        cache_specs = []
        for c, (win, dil) in zip(caches_t, GROUPS):
            assert c.shape[-1] == win, "the cache must hold one full window"
            cache_specs.append(pl.BlockSpec(
                (1, 1, 2, heads, HEAD_DIM, win),
                lambda i, j: (attn_layer, (i * nt + j) // parts, 0, (i * nt + j) % parts, 0, 0)))
        in_specs += [
            pl.BlockSpec((1,) + q.shape[1:], lambda i, j: ((i * nt + j) // parts, 0, 0, 0)),
            pl.BlockSpec((1,) + kvn.shape[1:], lambda i, j: ((i * nt + j) // parts, 0, 0, 0, 0)),
            *cache_specs,
            *[_const_spec(bt.shape) for bt in bts],
            _const_spec(b0.shape),
        ]
        args += [q, kvn, *caches_t, *bts, b0]
        if prev is not None:
            in_specs += [pl.BlockSpec(memory_space=pl.ANY)] * N_GROUPS
            aliases = {len(args) + k: 3 + k for k in range(N_GROUPS)}
            args += list(prev)
        out_specs += [pl.BlockSpec((1, HEAD_DIM, heads), lambda i, j: (i * nt + j, 0, 0)), *cache_specs]
        out_shape += [jax.ShapeDtypeStruct((steps, HEAD_DIM, heads), F32)]
        out_shape += [jax.ShapeDtypeStruct(c.shape, c.dtype) for c in caches_t]
    scratch = [pltpu.VMEM((tm + SUBLANES, D_FF), F32)]
    if zero_fill:
        out_specs += [pl.BlockSpec(memory_space=pl.ANY)] * len(zero_fill)
        out_shape += [jax.ShapeDtypeStruct(shape, F32) for shape in zero_fill]
        cache_block = max((s[2:] for s in zero_fill if len(s) == 6), key=math.prod)
        row_slab = max(((s[0] * s[1] * s[2] // (b * nt), s[3]) for s in zero_fill if len(s) == 4), key=math.prod)
        n_copies = sum(s[0] * s[1] // (b * nt) if len(s) == 6 else 1 for s in zero_fill)
        scratch += [pltpu.VMEM(cache_block, F32), pltpu.VMEM(row_slab, F32), pltpu.SemaphoreType.DMA((n_copies,))]
    return pl.pallas_call(
        functools.partial(_ffn_body, tm=tm, nt=nt, steps=b * nt, has_pre=pre is not None,
                          n_sample_in=0 if sample is None else N_SAMPLE_IN, n_alias=len(aliases),
                          n_zero=len(zero_fill)),
        grid=(b, nt),
        in_specs=in_specs,
        out_specs=out_specs,
        out_shape=out_shape,
        input_output_aliases=aliases,
        scratch_shapes=scratch,
        compiler_params=_cparams(("arbitrary", "arbitrary")),
        name="ffn_prompt",
    )(*args)


def _qkv_body(x_ref, g_ref, w_ref, qg_ref, kg_ref, bd_ref, *refs, tm, n_alias):
    refs = refs[n_alias:]
    qkv_refs = refs[:3 * N_GROUPS]
    cache_refs = refs[3 * N_GROUPS:4 * N_GROUPS]
    h = _rms(x_ref[0], g_ref[...]).astype(BF)
    bd = bd_ref[...]
    for g, (win, dil) in reversed(list(enumerate(GROUPS))):
        c0 = g * 3 * ATTN_WIDTH
        q = _dot(h, w_ref[:, c0:c0 + ATTN_WIDTH])
        k = _dot(h, w_ref[:, c0 + ATTN_WIDTH:c0 + 2 * ATTN_WIDTH])
        v = _dot(h, w_ref[:, c0 + 2 * ATTN_WIDTH:c0 + 3 * ATTN_WIDTH])
        q = _head_rms(q, qg_ref[...], bd)
        k = _head_rms(k, kg_ref[...], bd)

        c_ref = cache_refs[g]
        rows = min(win, tm)
        c_ref[0, :, 0:ATTN_WIDTH] = k[tm - rows:, :]
        c_ref[0, :, ATTN_WIDTH:2 * ATTN_WIDTH] = v[tm - rows:, :]

        for o_ref, val in zip(qkv_refs[3 * g:3 * g + 3], (q, k, v)):
            if dil == 1:
                o_ref[0, 0, 0] = val.astype(BF)
            else:
                by_residue = pltpu.einshape("abc->bac", val.reshape(tm // dil, dil, ATTN_WIDTH))
                o_ref[0, 0] = by_residue.astype(BF)


def _qkv_prompt(x, layer, attn_layer, g, w_qkv, q_gain, k_gain, bd, prev_rows, tm=QKV_TILE):
    n_attn = w_qkv.shape[0]
    b, t, d = x.shape
    nt = t // tm
    tiles_per_super = SUPER // tm
    out_specs, out_shapes = [], []
    for win, dil in GROUPS:
        rows = tm // dil
        shp = (b, t // SUPER, dil, SUPER // dil, ATTN_WIDTH)
        spec = pl.BlockSpec(
            (1, 1, dil, rows, ATTN_WIDTH),
            lambda i, j: (i, j // tiles_per_super, 0, j % tiles_per_super, 0))
        out_specs += [spec] * 3
        out_shapes += [jax.ShapeDtypeStruct(shp, BF)] * 3
    for win, dil in GROUPS:
        keep = win
        if keep >= tm:
            first = nt - keep // tm
            spec = pl.BlockSpec((None, 1, tm, 2 * ATTN_WIDTH),
                                lambda i, j, first=first: (attn_layer, i, jnp.maximum(j - first, 0), 0))
        else:
            spec = pl.BlockSpec((None, 1, keep, 2 * ATTN_WIDTH), lambda i, j: (attn_layer, i, 0, 0))
        out_specs.append(spec)
        out_shapes.append(jax.ShapeDtypeStruct((n_attn, b, keep, 2 * ATTN_WIDTH), F32))
    n_in = 6
    aliases = {} if prev_rows is None else {n_in + k: 3 * N_GROUPS + k for k in range(N_GROUPS)}
    res = pl.pallas_call(
        functools.partial(_qkv_body, tm=tm, n_alias=len(aliases)),
        grid=(b, nt),
        in_specs=[
            pl.BlockSpec((1, tm, d), lambda i, j: (i, j, 0)),
            _layer_spec(g, layer),
            *[_layer_spec(a, attn_layer) for a in (w_qkv, q_gain, k_gain)],
            _const_spec((ATTN_WIDTH, ATTN_WIDTH)),
            *[pl.BlockSpec(memory_space=pl.ANY)] * len(aliases),
        ],
        out_specs=out_specs,
        out_shape=out_shapes,
        input_output_aliases=aliases,
        compiler_params=_cparams(("arbitrary", "arbitrary")),
        name="qkv_prompt",
    )(x, g, w_qkv, q_gain, k_gain, bd, *(prev_rows or ()))
    qkv = [r.reshape(b, t, ATTN_WIDTH) for r in res[:3 * N_GROUPS]]
    return qkv, res[3 * N_GROUPS:]


def _attn_body(*refs):
    qkv_refs = refs[:3 * N_GROUPS]
    tb_ref = refs[3 * N_GROUPS]
    o_ref = refs[3 * N_GROUPS + 1]
    out_s, lse_s = refs[3 * N_GROUPS + 2:]
    n_super = o_ref.shape[1] // SUPER
    blocks = SUPER // QBLK
    lane = lax.broadcasted_iota(jnp.int32, (QBLK, LANES), 1)
    first_head = lane < HEAD_DIM
    contract_last = (((1,), (1,)), ((), ()))
    zero = jnp.zeros((), BF)
    lane2 = lax.broadcasted_iota(jnp.int32, (2 * QBLK, LANES), 1)
    first_head2 = lane2 < HEAD_DIM
    ones_h0 = jnp.where(first_head2, 1.0, 0.0).astype(BF)
    ones_h1 = jnp.where(first_head2, 0.0, 1.0).astype(BF)

    for s in range(n_super):
        for g, (win, dil) in enumerate(GROUPS):
            q_ref, k_ref, v_ref = qkv_refs[3 * g:3 * g + 3]
            per_res = blocks // dil

            def trip(it, carry, s=s, g=g, dil=dil, per_res=per_res,
                     q_ref=q_ref, k_ref=k_ref, v_ref=v_ref):
                def logits_of(u):
                    n = it * ATTN_UNROLL + u
                    c = n % per_res
                    r = n // per_res
                    start = s * SUPER + n * QBLK
                    first = int(c == 0 and s == 0)
                    if c > 0:
                        prev = start - QBLK
                    elif s == 0:
                        prev = start
                    else:
                        prev = start - SUPER + SUPER // dil - QBLK
                    qb = q_ref[0, pl.ds(start, QBLK), :]
                    q2 = jnp.concatenate(
                        [jnp.where(first_head, qb, zero), jnp.where(first_head, zero, qb)], axis=0)
                    k2 = jnp.concatenate(
                        [k_ref[0, pl.ds(prev, QBLK), :], k_ref[0, pl.ds(start, QBLK), :]], axis=0)
                    logits = lax.dot_general(q2, k2, contract_last, preferred_element_type=F32)
                    return logits + tb_ref[g, first, 0], start, prev, r + dil * QBLK * c

                def probs_of(logits, start, prev, nat):
                    m = jnp.max(logits, axis=1, keepdims=True)
                    return jnp.exp2(logits - m).astype(BF), m, start, prev, nat

                def finish(e, m, start, prev, nat):
                    v2 = jnp.concatenate(
                        [v_ref[0, pl.ds(prev, QBLK), :], v_ref[0, pl.ds(start, QBLK), :]], axis=0)
                    rhs = jnp.concatenate([
                        jnp.concatenate([jnp.where(first_head2, v2, zero), ones_h0], axis=1),
                        jnp.concatenate([jnp.where(first_head2, zero, v2), ones_h1], axis=1)], axis=0)
                    pv = _dot(jnp.concatenate([e[0:QBLK], e[QBLK:]], axis=1), rhs)
                    rows = pl.ds(nat, QBLK) if dil == 1 else pl.ds(nat, QBLK, stride=dil)
                    row_sum = pv[:, LANES:]
                    out_s[s, g, rows, :] = pv[:, 0:LANES] / row_sum
                    lse_s[s, g, rows, :] = jnp.where(first_head, m[0:QBLK], m[QBLK:]) + jnp.log2(row_sum)

                staged, probs = {}, {}
                for u in range(ATTN_UNROLL + 2):
                    if u < ATTN_UNROLL:
                        staged[u] = logits_of(u)
                    if 1 <= u <= ATTN_UNROLL:
                        probs[u - 1] = probs_of(*staged.pop(u - 1))
                    if u >= 2:
                        finish(*probs.pop(u - 2))
                return carry

            for it in range(blocks // ATTN_UNROLL):
                trip(it, 0)

        for i in range(SUPER // MERGE_ROWS):
            rows = slice(i * MERGE_ROWS, (i + 1) * MERGE_ROWS)
            l0, l1, l2 = lse_s[s, 0, rows, :], lse_s[s, 1, rows, :], lse_s[s, 2, rows, :]
            top = jnp.maximum(jnp.maximum(l0, l1), l2)
            w0, w1, w2 = jnp.exp2(l0 - top), jnp.exp2(l1 - top), jnp.exp2(l2 - top)
            num = w0 * out_s[s, 0, rows, :] + w1 * out_s[s, 1, rows, :] + w2 * out_s[s, 2, rows, :]
            o_ref[0, s * SUPER + i * MERGE_ROWS:s * SUPER + (i + 1) * MERGE_ROWS, :] = num / (w0 + w1 + w2)


def _attn_prompt(qkv, tables):
    b, t, _ = qkv[0].shape
    pairs = N_HEADS // 2
    col = lambda i, j: (i, 0, j)
    return pl.pallas_call(
        _attn_body,
        grid=(b, pairs),
        in_specs=[pl.BlockSpec((1, t, LANES), col)] * (3 * N_GROUPS) + [
            pl.BlockSpec((N_GROUPS, 2, 1, 2 * QBLK, 2 * QBLK), lambda i, j: (0, 0, j, 0, 0)),
        ],
        out_specs=pl.BlockSpec((1, t, LANES), col),
        out_shape=jax.ShapeDtypeStruct((b, t, ATTN_WIDTH), F32),
        scratch_shapes=[pltpu.VMEM((t // SUPER, N_GROUPS, SUPER, LANES), F32)] * 2,
        compiler_params=_cparams(("arbitrary", "arbitrary")),
        name="attn_prompt",
    )(*qkv, tables)


def _mixer_sample_body(x_ref, g_ref, win_ref, cw_ref, wout_ref, h0_ref, h1_ref, o_ref, u_ref):
    x = x_ref[...]
    h = _rms(x, g_ref[...]).astype(BF)
    bg = _dot(h, win_ref[:, 0:D_MODEL])
    cg = _dot(h, win_ref[:, D_MODEL:2 * D_MODEL])
    xv = _dot(h, win_ref[:, 2 * D_MODEL:3 * D_MODEL])
    u = cg * xv
    conv = cw_ref[0:1, :] * h0_ref[...]
    conv = conv + cw_ref[1:2, :] * h1_ref[...]
    conv = conv + cw_ref[2:3, :] * u
    o_ref[...] = x + _dot((bg * conv).astype(BF), wout_ref[...])
    u_ref[...] = u


def _mixer_sample(x, layer, mix_layer, g, w_in, conv_w, w_out, h0, h1):
    n, d = x.shape
    return pl.pallas_call(
        _mixer_sample_body,
        grid=(1,),
        in_specs=[_const_spec(x.shape), _layer_spec(g, layer),
                  *[_layer_spec(a, mix_layer) for a in (w_in, conv_w, w_out)],
                  _const_spec(h0.shape), _const_spec(h1.shape)],
        out_specs=[_whole_spec((n, d))] * 2,
        out_shape=[jax.ShapeDtypeStruct((n, d), F32)] * 2,
        compiler_params=_cparams(("arbitrary",)),
        name="mixer_sample",
    )(x, g, w_in, conv_w, w_out, h0, h1)


def _ffn_sample_body(*refs, has_pre):
    if has_pre:
        x_ref, a_ref, wa_ref = refs[:3]
        refs = refs[3:]
    else:
        x_ref = refs[0]
        refs = refs[1:]
    g_ref, wg_ref, wu_ref, cw_ref, cb_ref, wd_ref, h0_ref, h1_ref, o_ref, gate_ref = refs
    x = x_ref[...]
    if has_pre:
        x = x + _dot(a_ref[...].astype(BF), wa_ref[...])
    h = _rms(x, g_ref[...]).astype(BF)
    gate = _dot(h, wg_ref[...])
    up = _dot(h, wu_ref[...])
    z = cw_ref[0:1, :] * h0_ref[...]
    z = z + cw_ref[1:2, :] * h1_ref[...]
    z = z + cw_ref[2:3, :] * gate
    z = z + cb_ref[...]
    act = _silu(z) * up
    o_ref[...] = x + _dot(act.astype(BF), wd_ref[...])
    gate_ref[...] = gate


def _ffn_sample(x, pre, layer, g, w_gate, w_up, conv_w, conv_b, w_down, h0, h1):
    n, d = x.shape
    args, in_specs = [x], [_const_spec(x.shape)]
    if pre is not None:
        a, wa, pre_layer = pre
        args += [a, wa]
        in_specs += [_const_spec(a.shape), _layer_spec(wa, pre_layer)]
    weights = (g, w_gate, w_up, conv_w, conv_b, w_down)
    args += [*weights, h0, h1]
    in_specs += [*[_layer_spec(a, layer) for a in weights], _const_spec(h0.shape), _const_spec(h1.shape)]
    return pl.pallas_call(
        functools.partial(_ffn_sample_body, has_pre=pre is not None),
        grid=(1,),
        in_specs=in_specs,
        out_specs=[_whole_spec((n, d)), _whole_spec((n, D_FF))],
        out_shape=[jax.ShapeDtypeStruct((n, d), F32), jax.ShapeDtypeStruct((n, D_FF), F32)],
        compiler_params=_cparams(("arbitrary",)),
        name="ffn_sample",
    )(*args)


def _qkv_sample_body(x_ref, g_ref, w_ref, qg_ref, kg_ref, bd_ref, q_ref, kv_ref):
    h = _rms(x_ref[...], g_ref[...]).astype(BF)
    bd = bd_ref[...]
    for g in range(N_GROUPS):
        c0 = g * 3 * ATTN_WIDTH
        q = _dot(h, w_ref[:, c0:c0 + ATTN_WIDTH])
        k = _dot(h, w_ref[:, c0 + ATTN_WIDTH:c0 + 2 * ATTN_WIDTH])
        v = _dot(h, w_ref[:, c0 + 2 * ATTN_WIDTH:c0 + 3 * ATTN_WIDTH])
        q_ref[:, g * ATTN_WIDTH:(g + 1) * ATTN_WIDTH] = _head_rms(q, qg_ref[...], bd)
        kv_ref[:, 2 * g * ATTN_WIDTH:(2 * g + 1) * ATTN_WIDTH] = _head_rms(k, kg_ref[...], bd)
        kv_ref[:, (2 * g + 1) * ATTN_WIDTH:(2 * g + 2) * ATTN_WIDTH] = v


def _qkv_sample(x, layer, attn_layer, g, w_qkv, q_gain, k_gain, bd):
    n = x.shape[0]
    out_cols = (N_GROUPS * ATTN_WIDTH, N_GROUPS * 2 * ATTN_WIDTH)
    return pl.pallas_call(
        _qkv_sample_body,
        grid=(1,),
        in_specs=[_const_spec(x.shape), _layer_spec(g, layer),
                  *[_layer_spec(a, attn_layer) for a in (w_qkv, q_gain, k_gain)], _const_spec(bd.shape)],
        out_specs=[_whole_spec((n, c)) for c in out_cols],
        out_shape=[jax.ShapeDtypeStruct((n, c), F32) for c in out_cols],
        compiler_params=_cparams(("arbitrary",)),
        name="qkv_sample",
    )(x, g, w_qkv, q_gain, k_gain, bd)


def _t5_bucket(dist):
    exact = N_BUCKETS // 2
    n = np.asarray(dist, dtype=np.float32)
    large = exact + np.log(np.maximum(n, 1.0) / exact) / math.log(MAX_DISTANCE / exact) * (N_BUCKETS - exact)
    large = np.minimum(np.floor(large), N_BUCKETS - 1)
    return np.where(n < exact, n, large).astype(np.int32)


def _bias_tables(rel_bias):
    ci = np.arange(2 * QBLK)[None, None, :]
    period = 3 * QBLK
    tabs, bts, b0s = [], [], []
    for g, (win, dil) in enumerate(GROUPS):
        bucket = _t5_bucket(dil * np.arange(N_KEYS + 1))
        bias = jnp.take(rel_bias, bucket, axis=0)[:, g * N_HEADS:(g + 1) * N_HEADS].T
        bias = bias * LOG2E
        w = jnp.concatenate([bias[:, ::-1], jnp.full((N_HEADS, period - N_KEYS - 1), NEG, F32)], axis=1)
        band = jnp.tile(w, (1, QBLK))[:, :QBLK * (period - 1)]
        band = band.reshape(N_HEADS, QBLK, period - 1)[:, :, :2 * QBLK]
        band_first = jnp.where(ci >= QBLK, band, NEG)
        pair_rows = (N_HEADS // 2, 2 * QBLK, 2 * QBLK)
        tabs.append(jnp.stack([band.reshape(pair_rows), band_first.reshape(pair_rows)], axis=0))
        vals = bias[:, N_KEYS:0:-1][:, :, None]
        skipped = jnp.full((N_HEADS, N_KEYS, dil - 1), NEG, F32)
        bts.append(jnp.concatenate([vals, skipped], axis=2).reshape(N_HEADS, 1, win))
        b0s.append(bias[:, 0:1][:, :, None])
    return jnp.stack(tabs), bts, jnp.stack(b0s)


def kernel(x_prompt, x_sample, state_sc_conv, cache_kv_d1, cache_kv_d4, cache_kv_d16, state_ffn_conv,
           norm_mix, norm_ffn, sc_w_in, sc_conv_w, sc_w_out, attn_w_qkv, attn_q_norm, attn_k_norm,
           attn_w_out, rel_bias, ffn_w_gate, ffn_w_up, ffn_conv_w, ffn_conv_b, ffn_w_down):
    depth = norm_mix.shape[0]
    nb = x_sample.shape[0]
    caches_t = [jnp.transpose(c, (0, 1, 3, 4, 5, 2)) for c in (cache_kv_d1, cache_kv_d4, cache_kv_d16)]
    tables, bts, b0 = _bias_tables(rel_bias)
    seg = np.arange(ATTN_WIDTH) // HEAD_DIM
    bd = jnp.asarray(seg[:, None] == seg[None, :], BF)

    mixer_w = (sc_w_in.astype(BF), sc_conv_w, sc_w_out.astype(BF))
    w_qkv_b, w_ao_b = attn_w_qkv.astype(BF), attn_w_out.astype(BF)
    ffn_w = (norm_ffn[:, None, :], ffn_w_gate.astype(BF), ffn_w_up.astype(BF), ffn_conv_w,
             ffn_conv_b[:, None, :], ffn_w_down.astype(BF))
    g_mix = norm_mix[:, None, :]
    q_gain = (jnp.tile(attn_q_norm, (1, N_HEADS)) * (SCALE * LOG2E))[:, None, :]
    k_gain = jnp.tile(attn_k_norm, (1, N_HEADS))[:, None, :]

    xp = x_prompt
    xs = x_sample[:, 0, :]
    p_sc, s_sc, p_ffn, s_ffn = [], [], [], []
    kv_rows = new_caches = None
    for i in range(depth):
        j = i // 2
        sf = state_ffn_conv[i]
        if i % 2 == 0:
            xp, hist = _mixer_prompt(xp, i, j, g_mix, *mixer_w)
            p_sc.append(hist)
            st = state_sc_conv[j]
            xs, u = _mixer_sample(xs, i, j, g_mix, *mixer_w, st[:, 0], st[:, 1])
            s_sc.append(jnp.stack([st[:, 1], u], axis=1))
            pre_p = pre_s = None
        else:
            qkv, kv_rows = _qkv_prompt(xp, i, j, g_mix, w_qkv_b, q_gain, k_gain, bd, kv_rows)
            pre_p = (_attn_prompt(qkv, tables), w_ao_b, j)

            qs, kvs = _qkv_sample(xs, i, j, g_mix, w_qkv_b, q_gain, k_gain, bd)
            qs = jnp.swapaxes(qs.reshape(nb, N_GROUPS, N_HEADS, HEAD_DIM), -1, -2)
            kvs = jnp.swapaxes(kvs.reshape(nb, N_GROUPS, 2, N_HEADS, HEAD_DIM), -1, -2)
            xp, fh, a_s, *new_caches = _ffn_prompt(
                xp, pre_p, i, *ffn_w, sample=(qs, kvs, caches_t, j, bts, b0, new_caches))
            a_s = jnp.swapaxes(a_s.reshape(nb, -1, HEAD_DIM, a_s.shape[-1]), -1, -2)
            pre_s = (a_s.reshape(nb, ATTN_WIDTH), w_ao_b, j)
        if i == 0:
            n_attn = attn_w_qkv.shape[0]
            fills = [c.shape for c in caches_t]
            fills += [(n_attn, xp.shape[0], win, 2 * ATTN_WIDTH) for win, _ in GROUPS]
            xp, fh, *zeros = _ffn_prompt(xp, pre_p, i, *ffn_w, zero_fill=fills)
            new_caches, kv_rows = zeros[:N_GROUPS], zeros[N_GROUPS:]
        elif i % 2 == 0:
            xp, fh = _ffn_prompt(xp, pre_p, i, *ffn_w)
        p_ffn.append(fh)
        xs, gate = _ffn_sample(xs, pre_s, i, *ffn_w, sf[:, 0], sf[:, 1])
        s_ffn.append(jnp.stack([sf[:, 1], gate], axis=1))

    s_kv = [jnp.transpose(c, (0, 1, 5, 2, 3, 4)) for c in new_caches]
    p_kv = [r.reshape(r.shape[:3] + (2, N_HEADS, HEAD_DIM)) for r in kv_rows]
    return (xp, xs[:, None, :],
            jnp.stack(p_sc, axis=0), p_kv[0], p_kv[1], p_kv[2], jnp.stack(p_ffn, axis=0),
            jnp.stack(s_sc, axis=0), s_kv[0], s_kv[1], s_kv[2], jnp.stack(s_ffn, axis=0))
```

```python
import functools
import math

import numpy as np
import jax
import jax.numpy as jnp
from jax import lax
from jax.experimental import pallas as pl
from jax.experimental.pallas import tpu as pltpu

D_MODEL = 1024
D_FF = 2816
N_HEADS = 8
HEAD_DIM = 64
ATTN_WIDTH = N_HEADS * HEAD_DIM
GROUPS = ((128, 1), (512, 4), (2048, 16))
N_GROUPS = len(GROUPS)
N_KEYS = 128
N_BUCKETS = 32
MAX_DISTANCE = 2048
EPS = 1e-6
NEG = -1e30
SCALE = HEAD_DIM ** -0.5
LOG2E = math.log2(math.e)
SUPER = 2048
QBLK = 128
LANES = 128
SUBLANES = 8
ATTN_UNROLL = 16
MERGE_ROWS = 256
MIXER_TILE = 1024
QKV_TILE = 512
FFN_TILE = 512
FFN_TILE_WITH_SAMPLE = 256
BF = jnp.bfloat16
F32 = jnp.float32

V7X_VMEM_BYTES = 64 * 1024 * 1024
VMEM_LIMIT = V7X_VMEM_BYTES - 8 * 1024 * 1024


def _cparams(sem):
    return pltpu.CompilerParams(dimension_semantics=sem, vmem_limit_bytes=VMEM_LIMIT)


def _const_spec(shape):
    nd = len(shape)
    return pl.BlockSpec(shape, lambda *_: (0,) * nd, pipeline_mode=pl.Buffered(1))


def _whole_spec(shape):
    nd = len(shape)
    return pl.BlockSpec(shape, lambda *_: (0,) * nd)


def _layer_spec(arr, layer):
    nd = arr.ndim
    return pl.BlockSpec((None,) + arr.shape[1:], lambda *_: (layer,) + (0,) * (nd - 1),
                        pipeline_mode=pl.Buffered(1))


def _dot(a, b):
    return jnp.dot(a, b, preferred_element_type=F32)


def _rms(x, g):
    r = lax.rsqrt(jnp.mean(x * x, axis=-1, keepdims=True) + EPS)
    return (x * r) * g


def _silu(z):
    half = 0.5 * z
    return half + half * jnp.tanh(half)


def _head_rms(x, gain, bd):
    slab = 2 * LANES
    out = []
    for c in range(0, x.shape[1], slab):
        xs = x[:, c:c + slab]
        ss = _dot((xs * xs).astype(BF), bd[0:slab, 0:slab])
        r = lax.rsqrt(ss * (1.0 / HEAD_DIM) + EPS)
        out.append((xs * r) * gain[:, c:c + slab])
    return jnp.concatenate(out, axis=1)


def _conv3(buf, cw_ref, tm):
    ext = buf[0:tm + SUBLANES, :]
    back2 = pltpu.roll(ext, 2, axis=0)[SUBLANES:, :]
    back1 = pltpu.roll(ext, 1, axis=0)[SUBLANES:, :]
    out = cw_ref[0:1, :] * back2
    out = out + cw_ref[1:2, :] * back1
    return out + cw_ref[2:3, :] * ext[SUBLANES:, :]


def _carry_rows(buf, hist_ref, tm):
    tail = buf[tm:tm + SUBLANES, :]
    buf[0:SUBLANES, :] = tail
    hist_ref[0] = tail[SUBLANES - 2:, :]


def _mixer_body(x_ref, g_ref, win_ref, cw_ref, wout_ref, o_ref, hist_ref, ubuf, *, tm):
    t = pl.program_id(1)

    @pl.when(t == 0)
    def _():
        ubuf[0:SUBLANES, :] = jnp.zeros((SUBLANES, D_MODEL), F32)

    x = x_ref[0]
    h = _rms(x, g_ref[...]).astype(BF)
    bg = _dot(h, win_ref[:, 0:D_MODEL])
    cg = _dot(h, win_ref[:, D_MODEL:2 * D_MODEL])
    xv = _dot(h, win_ref[:, 2 * D_MODEL:3 * D_MODEL])
    ubuf[SUBLANES:SUBLANES + tm, :] = cg * xv
    conv = _conv3(ubuf, cw_ref, tm)
    y = _dot((bg * conv).astype(BF), wout_ref[...])
    o_ref[0] = x + y
    _carry_rows(ubuf, hist_ref, tm)


def _mixer_prompt(x, layer, mix_layer, g, w_in, conv_w, w_out, tm=MIXER_TILE):
    b, t, d = x.shape
    nt = t // tm
    return pl.pallas_call(
        functools.partial(_mixer_body, tm=tm),
        grid=(b, nt),
        in_specs=[
            pl.BlockSpec((1, tm, d), lambda i, j: (i, j, 0)),
            _layer_spec(g, layer),
            *[_layer_spec(a, mix_layer) for a in (w_in, conv_w, w_out)],
        ],
        out_specs=[
            pl.BlockSpec((1, tm, d), lambda i, j: (i, j, 0)),
            pl.BlockSpec((1, 2, d), lambda i, j: (i, 0, 0)),
        ],
        out_shape=[
            jax.ShapeDtypeStruct((b, t, d), F32),
            jax.ShapeDtypeStruct((b, 2, d), F32),
        ],
        scratch_shapes=[pltpu.VMEM((tm + SUBLANES, d), F32)],
        compiler_params=_cparams(("arbitrary", "arbitrary")),
        name="mixer_prompt",
    )(x, g, w_in, conv_w, w_out)


def _sample_heads(first_head, q_ref, kvn_ref, c_refs, bt_refs, b0_ref, o_ref, n_refs):
    heads = o_ref.shape[-1]
    head_lane = lax.broadcasted_iota(jnp.int32, (HEAD_DIM, N_HEADS), 1)
    out_lane = lax.broadcasted_iota(jnp.int32, (HEAD_DIM, heads), 1)
    out = jnp.zeros((HEAD_DIM, heads), F32)
    for hh in range(heads):
        h = first_head + hh
        mine = head_lane == h
        column = lambda x: jnp.sum(jnp.where(mine, x, 0.0), axis=1, keepdims=True)
        ms, ss, nums = [], [], []
        for g in range(N_GROUPS):
            c_ref, n_ref = c_refs[g], n_refs[g]
            keep = c_ref.shape[-1]
            qc = column(q_ref[0, g])
            kn = column(kvn_ref[0, g, 0])
            vn = column(kvn_ref[0, g, 1])
            kt = c_ref[0, 0, 0, hh]
            vt = c_ref[0, 0, 1, hh]
            l = jnp.sum(kt * qc, axis=0, keepdims=True) + bt_refs[g][h]
            ln = jnp.sum(kn * qc, axis=0, keepdims=True) + b0_ref[g, h]
            m = jnp.maximum(jnp.max(l, axis=1, keepdims=True), ln)
            e = jnp.exp2(l - m)
            en = jnp.exp2(ln - m)
            ss.append(jnp.sum(e, axis=1, keepdims=True) + en)
            nums.append(jnp.sum(vt * e, axis=1, keepdims=True) + vn * en)
            ms.append(m)
            n_ref[0, 0, 0, hh] = pltpu.roll(kt, keep - 1, axis=1)
            n_ref[0, 0, 1, hh] = pltpu.roll(vt, keep - 1, axis=1)
            n_ref[0, 0, 0, hh, :, keep - 1:keep] = kn
            n_ref[0, 0, 1, hh, :, keep - 1:keep] = vn
        mm = jnp.maximum(jnp.maximum(ms[0], ms[1]), ms[2])
        w = [jnp.exp2(m - mm) for m in ms]
        den = w[0] * ss[0] + w[1] * ss[1] + w[2] * ss[2]
        num = w[0] * nums[0] + w[1] * nums[1] + w[2] * nums[2]
        out = jnp.where(out_lane == hh, num / den, out)
    o_ref[0] = out


N_SAMPLE_IN = 9


def _ffn_body(*refs, tm, nt, steps, has_pre, n_sample_in, n_alias, n_zero):
    if has_pre:
        x_ref, a_ref, wa_ref = refs[:3]
        refs = refs[3:]
    else:
        x_ref = refs[0]
        refs = refs[1:]
    g_ref, wg_ref, wu_ref, cw_ref, cb_ref, wd_ref = refs[:6]
    sample_in = refs[6:6 + n_sample_in]
    n_scratch = 4 if n_zero else 1
    outs = refs[6 + n_sample_in + n_alias:-n_scratch]
    o_ref, hist_ref, sample_out = outs[0], outs[1], outs[2:]
    gbuf = refs[-n_scratch]
    t = pl.program_id(1)
    step = pl.program_id(0) * nt + t

    @pl.when(t == 0)
    def _():
        gbuf[0:SUBLANES, :] = jnp.zeros((SUBLANES, D_FF), F32)

    zero_copies = []
    if n_zero:
        zc, zr, zsem = refs[-3:]

        @pl.when(step == 0)
        def _():
            zc[...] = jnp.zeros(zc.shape, F32)
            zr[...] = jnp.zeros(zr.shape, F32)

        for z_ref in outs[2:2 + n_zero]:
            if len(z_ref.shape) == 6:
                n_l, n_b, keep = z_ref.shape[0], z_ref.shape[1], z_ref.shape[-1]
                for k in range(n_l * n_b // steps):
                    blk = step * (n_l * n_b // steps) + k
                    zero_copies.append(pltpu.make_async_copy(
                        zc.at[:, :, :, pl.ds(0, keep)], z_ref.at[blk // n_b, blk % n_b],
                        zsem.at[len(zero_copies)]))
            else:
                n_l, n_b, win = z_ref.shape[0], z_ref.shape[1], z_ref.shape[2]
                per_row = steps // (n_l * n_b)
                rows = win // per_row
                zero_copies.append(pltpu.make_async_copy(
                    zr.at[pl.ds(0, rows), :],
                    z_ref.at[step // (n_b * per_row), (step // per_row) % n_b,
                             pl.ds((step % per_row) * rows, rows), :],
                    zsem.at[len(zero_copies)]))
        for cp in zero_copies:
            cp.start()

    if n_sample_in:
        q_ref, kvn_ref = sample_in[0:2]
        heads = sample_out[0].shape[-1]
        _sample_heads((step % (N_HEADS // heads)) * heads, q_ref, kvn_ref, sample_in[2:5], sample_in[5:8],
                      sample_in[8], sample_out[0], sample_out[1:4])
    x = x_ref[0]
    if has_pre:
        x = x + _dot(a_ref[0].astype(BF), wa_ref[...])
    h = _rms(x, g_ref[...]).astype(BF)
    gbuf[SUBLANES:SUBLANES + tm, :] = _dot(h, wg_ref[...])
    up = _dot(h, wu_ref[...])
    z = _conv3(gbuf, cw_ref, tm) + cb_ref[...]
    act = _silu(z) * up
    o_ref[0] = x + _dot(act.astype(BF), wd_ref[...])
    _carry_rows(gbuf, hist_ref, tm)
    for cp in zero_copies:
        cp.wait()


def _ffn_prompt(x, pre, layer, g, w_gate, w_up, conv_w, conv_b, w_down, sample=None, zero_fill=()):
    b, t, d = x.shape
    tm = FFN_TILE if sample is None else FFN_TILE_WITH_SAMPLE
    nt = t // tm
    row = lambda i, j: (i, j, 0)
    in_specs = [pl.BlockSpec((1, tm, d), row)]
    args = [x]
    if pre is not None:
        a, wa, pre_layer = pre
        in_specs += [pl.BlockSpec((1, tm, a.shape[-1]), row), _layer_spec(wa, pre_layer)]
        args += [a, wa]
    weights = (g, w_gate, w_up, conv_w, conv_b, w_down)
    in_specs += [_layer_spec(a, layer) for a in weights]
    args += weights
    out_specs = [
        pl.BlockSpec((1, tm, d), row),
        pl.BlockSpec((1, 2, D_FF), lambda i, j: (i, 0, 0)),
    ]
    out_shape = [
        jax.ShapeDtypeStruct((b, t, d), F32),
        jax.ShapeDtypeStruct((b, 2, D_FF), F32),
    ]
    aliases = {}
    if sample is not None:
        q, kvn, caches_t, attn_layer, bts, b0, prev = sample
        nb = q.shape[0]
        steps = b * nt
        parts = steps // nb
        heads = N_HEADS // parts
        assert parts * nb == steps and heads * parts == N_HEADS
        cache_specs = []
        for c, (win, dil) in zip(caches_t, GROUPS):
            assert c.shape[-1] == win, "the cache must hold one full window"
            cache_specs.append(pl.BlockSpec(
                (1, 1, 2, heads, HEAD_DIM, win),
                lambda i, j: (attn_layer, (i * nt + j) // parts, 0, (i * nt + j) % parts, 0, 0)))
        in_specs += [
            pl.BlockSpec((1,) + q.shape[1:], lambda i, j: ((i * nt + j) // parts, 0, 0, 0)),
            pl.BlockSpec((1,) + kvn.shape[1:], lambda i, j: ((i * nt + j) // parts, 0, 0, 0, 0)),
            *cache_specs,
            *[_const_spec(bt.shape) for bt in bts],
            _const_spec(b0.shape),
        ]
        args += [q, kvn, *caches_t, *bts, b0]
        if prev is not None:
            in_specs += [pl.BlockSpec(memory_space=pl.ANY)] * N_GROUPS
            aliases = {len(args) + k: 3 + k for k in range(N_GROUPS)}
            args += list(prev)
        out_specs += [pl.BlockSpec((1, HEAD_DIM, heads), lambda i, j: (i * nt + j, 0, 0)), *cache_specs]
        out_shape += [jax.ShapeDtypeStruct((steps, HEAD_DIM, heads), F32)]
        out_shape += [jax.ShapeDtypeStruct(c.shape, c.dtype) for c in caches_t]
    scratch = [pltpu.VMEM((tm + SUBLANES, D_FF), F32)]
    if zero_fill:
        out_specs += [pl.BlockSpec(memory_space=pl.ANY)] * len(zero_fill)
        out_shape += [jax.ShapeDtypeStruct(shape, F32) for shape in zero_fill]
        cache_block = max((s[2:] for s in zero_fill if len(s) == 6), key=math.prod)
        row_slab = max(((s[0] * s[1] * s[2] // (b * nt), s[3]) for s in zero_fill if len(s) == 4), key=math.prod)
        n_copies = sum(s[0] * s[1] // (b * nt) if len(s) == 6 else 1 for s in zero_fill)
        scratch += [pltpu.VMEM(cache_block, F32), pltpu.VMEM(row_slab, F32), pltpu.SemaphoreType.DMA((n_copies,))]
    return pl.pallas_call(
        functools.partial(_ffn_body, tm=tm, nt=nt, steps=b * nt, has_pre=pre is not None,
                          n_sample_in=0 if sample is None else N_SAMPLE_IN, n_alias=len(aliases),
                          n_zero=len(zero_fill)),
        grid=(b, nt),
        in_specs=in_specs,
        out_specs=out_specs,
        out_shape=out_shape,
        input_output_aliases=aliases,
        scratch_shapes=scratch,
        compiler_params=_cparams(("arbitrary", "arbitrary")),
        name="ffn_prompt",
    )(*args)


def _qkv_body(x_ref, g_ref, w_ref, qg_ref, kg_ref, bd_ref, *refs, tm, n_alias):
    refs = refs[n_alias:]
    qkv_refs = refs[:3 * N_GROUPS]
    cache_refs = refs[3 * N_GROUPS:4 * N_GROUPS]
    h = _rms(x_ref[0], g_ref[...]).astype(BF)
    bd = bd_ref[...]
    for g, (win, dil) in reversed(list(enumerate(GROUPS))):
        c0 = g * 3 * ATTN_WIDTH
        q = _dot(h, w_ref[:, c0:c0 + ATTN_WIDTH])
        k = _dot(h, w_ref[:, c0 + ATTN_WIDTH:c0 + 2 * ATTN_WIDTH])
        v = _dot(h, w_ref[:, c0 + 2 * ATTN_WIDTH:c0 + 3 * ATTN_WIDTH])
        q = _head_rms(q, qg_ref[...], bd)
        k = _head_rms(k, kg_ref[...], bd)

        c_ref = cache_refs[g]
        rows = min(win, tm)
        c_ref[0, :, 0:ATTN_WIDTH] = k[tm - rows:, :]
        c_ref[0, :, ATTN_WIDTH:2 * ATTN_WIDTH] = v[tm - rows:, :]

        for o_ref, val in zip(qkv_refs[3 * g:3 * g + 3], (q, k, v)):
            if dil == 1:
                o_ref[0, 0, 0] = val.astype(BF)
            else:
                by_residue = pltpu.einshape("abc->bac", val.reshape(tm // dil, dil, ATTN_WIDTH))
                o_ref[0, 0] = by_residue.astype(BF)


def _qkv_prompt(x, layer, attn_layer, g, w_qkv, q_gain, k_gain, bd, prev_rows, tm=QKV_TILE):
    n_attn = w_qkv.shape[0]
    b, t, d = x.shape
    nt = t // tm
    tiles_per_super = SUPER // tm
    out_specs, out_shapes = [], []
    for win, dil in GROUPS:
        rows = tm // dil
        shp = (b, t // SUPER, dil, SUPER // dil, ATTN_WIDTH)
        spec = pl.BlockSpec(
            (1, 1, dil, rows, ATTN_WIDTH),
            lambda i, j: (i, j // tiles_per_super, 0, j % tiles_per_super, 0))
        out_specs += [spec] * 3
        out_shapes += [jax.ShapeDtypeStruct(shp, BF)] * 3
    for win, dil in GROUPS:
        keep = win
        if keep >= tm:
            first = nt - keep // tm
            spec = pl.BlockSpec((None, 1, tm, 2 * ATTN_WIDTH),
                                lambda i, j, first=first: (attn_layer, i, jnp.maximum(j - first, 0), 0))
        else:
            spec = pl.BlockSpec((None, 1, keep, 2 * ATTN_WIDTH), lambda i, j: (attn_layer, i, 0, 0))
        out_specs.append(spec)
        out_shapes.append(jax.ShapeDtypeStruct((n_attn, b, keep, 2 * ATTN_WIDTH), F32))
    n_in = 6
    aliases = {} if prev_rows is None else {n_in + k: 3 * N_GROUPS + k for k in range(N_GROUPS)}
    res = pl.pallas_call(
        functools.partial(_qkv_body, tm=tm, n_alias=len(aliases)),
        grid=(b, nt),
        in_specs=[
            pl.BlockSpec((1, tm, d), lambda i, j: (i, j, 0)),
            _layer_spec(g, layer),
            *[_layer_spec(a, attn_layer) for a in (w_qkv, q_gain, k_gain)],
            _const_spec((ATTN_WIDTH, ATTN_WIDTH)),
            *[pl.BlockSpec(memory_space=pl.ANY)] * len(aliases),
        ],
        out_specs=out_specs,
        out_shape=out_shapes,
        input_output_aliases=aliases,
        compiler_params=_cparams(("arbitrary", "arbitrary")),
        name="qkv_prompt",
    )(x, g, w_qkv, q_gain, k_gain, bd, *(prev_rows or ()))
    qkv = [r.reshape(b, t, ATTN_WIDTH) for r in res[:3 * N_GROUPS]]
    return qkv, res[3 * N_GROUPS:]


def _attn_body(*refs):
    qkv_refs = refs[:3 * N_GROUPS]
    tb_ref = refs[3 * N_GROUPS]
    o_ref = refs[3 * N_GROUPS + 1]
    out_s, lse_s = refs[3 * N_GROUPS + 2:]
    n_super = o_ref.shape[1] // SUPER
    blocks = SUPER // QBLK
    lane = lax.broadcasted_iota(jnp.int32, (QBLK, LANES), 1)
    first_head = lane < HEAD_DIM
    contract_last = (((1,), (1,)), ((), ()))
    zero = jnp.zeros((), BF)
    lane2 = lax.broadcasted_iota(jnp.int32, (2 * QBLK, LANES), 1)
    first_head2 = lane2 < HEAD_DIM
    ones_h0 = jnp.where(first_head2, 1.0, 0.0).astype(BF)
    ones_h1 = jnp.where(first_head2, 0.0, 1.0).astype(BF)

    for s in range(n_super):
        for g, (win, dil) in enumerate(GROUPS):
            q_ref, k_ref, v_ref = qkv_refs[3 * g:3 * g + 3]
            per_res = blocks // dil

            def trip(it, carry, s=s, g=g, dil=dil, per_res=per_res,
                     q_ref=q_ref, k_ref=k_ref, v_ref=v_ref):
                def logits_of(u):
                    n = it * ATTN_UNROLL + u
                    c = n % per_res
                    r = n // per_res
                    start = s * SUPER + n * QBLK
                    first = int(c == 0 and s == 0)
                    if c > 0:
                        prev = start - QBLK
                    elif s == 0:
                        prev = start
                    else:
                        prev = start - SUPER + SUPER // dil - QBLK
                    qb = q_ref[0, pl.ds(start, QBLK), :]
                    q2 = jnp.concatenate(
                        [jnp.where(first_head, qb, zero), jnp.where(first_head, zero, qb)], axis=0)
                    k2 = jnp.concatenate(
                        [k_ref[0, pl.ds(prev, QBLK), :], k_ref[0, pl.ds(start, QBLK), :]], axis=0)
                    logits = lax.dot_general(q2, k2, contract_last, preferred_element_type=F32)
                    return logits + tb_ref[g, first, 0], start, prev, r + dil * QBLK * c

                def probs_of(logits, start, prev, nat):
                    m = jnp.max(logits, axis=1, keepdims=True)
                    return jnp.exp2(logits - m).astype(BF), m, start, prev, nat

                def finish(e, m, start, prev, nat):
                    v2 = jnp.concatenate(
                        [v_ref[0, pl.ds(prev, QBLK), :], v_ref[0, pl.ds(start, QBLK), :]], axis=0)
                    rhs = jnp.concatenate([
                        jnp.concatenate([jnp.where(first_head2, v2, zero), ones_h0], axis=1),
                        jnp.concatenate([jnp.where(first_head2, zero, v2), ones_h1], axis=1)], axis=0)
                    pv = _dot(jnp.concatenate([e[0:QBLK], e[QBLK:]], axis=1), rhs)
                    rows = pl.ds(nat, QBLK) if dil == 1 else pl.ds(nat, QBLK, stride=dil)
                    row_sum = pv[:, LANES:]
                    out_s[s, g, rows, :] = pv[:, 0:LANES] / row_sum
                    lse_s[s, g, rows, :] = jnp.where(first_head, m[0:QBLK], m[QBLK:]) + jnp.log2(row_sum)

                staged, probs = {}, {}
                for u in range(ATTN_UNROLL + 2):
                    if u < ATTN_UNROLL:
                        staged[u] = logits_of(u)
                    if 1 <= u <= ATTN_UNROLL:
                        probs[u - 1] = probs_of(*staged.pop(u - 1))
                    if u >= 2:
                        finish(*probs.pop(u - 2))
                return carry

            for it in range(blocks // ATTN_UNROLL):
                trip(it, 0)

        for i in range(SUPER // MERGE_ROWS):
            rows = slice(i * MERGE_ROWS, (i + 1) * MERGE_ROWS)
            l0, l1, l2 = lse_s[s, 0, rows, :], lse_s[s, 1, rows, :], lse_s[s, 2, rows, :]
            top = jnp.maximum(jnp.maximum(l0, l1), l2)
            w0, w1, w2 = jnp.exp2(l0 - top), jnp.exp2(l1 - top), jnp.exp2(l2 - top)
            num = w0 * out_s[s, 0, rows, :] + w1 * out_s[s, 1, rows, :] + w2 * out_s[s, 2, rows, :]
            o_ref[0, s * SUPER + i * MERGE_ROWS:s * SUPER + (i + 1) * MERGE_ROWS, :] = num / (w0 + w1 + w2)


def _attn_prompt(qkv, tables):
    b, t, _ = qkv[0].shape
    pairs = N_HEADS // 2
    col = lambda i, j: (i, 0, j)
    return pl.pallas_call(
        _attn_body,
        grid=(b, pairs),
        in_specs=[pl.BlockSpec((1, t, LANES), col)] * (3 * N_GROUPS) + [
            pl.BlockSpec((N_GROUPS, 2, 1, 2 * QBLK, 2 * QBLK), lambda i, j: (0, 0, j, 0, 0)),
        ],
        out_specs=pl.BlockSpec((1, t, LANES), col),
        out_shape=jax.ShapeDtypeStruct((b, t, ATTN_WIDTH), F32),
        scratch_shapes=[pltpu.VMEM((t // SUPER, N_GROUPS, SUPER, LANES), F32)] * 2,
        compiler_params=_cparams(("arbitrary", "arbitrary")),
        name="attn_prompt",
    )(*qkv, tables)


def _mixer_sample_body(x_ref, g_ref, win_ref, cw_ref, wout_ref, h0_ref, h1_ref, o_ref, u_ref):
    x = x_ref[...]
    h = _rms(x, g_ref[...]).astype(BF)
    bg = _dot(h, win_ref[:, 0:D_MODEL])
    cg = _dot(h, win_ref[:, D_MODEL:2 * D_MODEL])
    xv = _dot(h, win_ref[:, 2 * D_MODEL:3 * D_MODEL])
    u = cg * xv
    conv = cw_ref[0:1, :] * h0_ref[...]
    conv = conv + cw_ref[1:2, :] * h1_ref[...]
    conv = conv + cw_ref[2:3, :] * u
    o_ref[...] = x + _dot((bg * conv).astype(BF), wout_ref[...])
    u_ref[...] = u


def _mixer_sample(x, layer, mix_layer, g, w_in, conv_w, w_out, h0, h1):
    n, d = x.shape
    return pl.pallas_call(
        _mixer_sample_body,
        grid=(1,),
        in_specs=[_const_spec(x.shape), _layer_spec(g, layer),
                  *[_layer_spec(a, mix_layer) for a in (w_in, conv_w, w_out)],
                  _const_spec(h0.shape), _const_spec(h1.shape)],
        out_specs=[_whole_spec((n, d))] * 2,
        out_shape=[jax.ShapeDtypeStruct((n, d), F32)] * 2,
        compiler_params=_cparams(("arbitrary",)),
        name="mixer_sample",
    )(x, g, w_in, conv_w, w_out, h0, h1)


def _ffn_sample_body(*refs, has_pre):
    if has_pre:
        x_ref, a_ref, wa_ref = refs[:3]
        refs = refs[3:]
    else:
        x_ref = refs[0]
        refs = refs[1:]
    g_ref, wg_ref, wu_ref, cw_ref, cb_ref, wd_ref, h0_ref, h1_ref, o_ref, gate_ref = refs
    x = x_ref[...]
    if has_pre:
        x = x + _dot(a_ref[...].astype(BF), wa_ref[...])
    h = _rms(x, g_ref[...]).astype(BF)
    gate = _dot(h, wg_ref[...])
    up = _dot(h, wu_ref[...])
    z = cw_ref[0:1, :] * h0_ref[...]
    z = z + cw_ref[1:2, :] * h1_ref[...]
    z = z + cw_ref[2:3, :] * gate
    z = z + cb_ref[...]
    act = _silu(z) * up
    o_ref[...] = x + _dot(act.astype(BF), wd_ref[...])
    gate_ref[...] = gate


def _ffn_sample(x, pre, layer, g, w_gate, w_up, conv_w, conv_b, w_down, h0, h1):
    n, d = x.shape
    args, in_specs = [x], [_const_spec(x.shape)]
    if pre is not None:
        a, wa, pre_layer = pre
        args += [a, wa]
        in_specs += [_const_spec(a.shape), _layer_spec(wa, pre_layer)]
    weights = (g, w_gate, w_up, conv_w, conv_b, w_down)
    args += [*weights, h0, h1]
    in_specs += [*[_layer_spec(a, layer) for a in weights], _const_spec(h0.shape), _const_spec(h1.shape)]
    return pl.pallas_call(
        functools.partial(_ffn_sample_body, has_pre=pre is not None),
        grid=(1,),
        in_specs=in_specs,
        out_specs=[_whole_spec((n, d)), _whole_spec((n, D_FF))],
        out_shape=[jax.ShapeDtypeStruct((n, d), F32), jax.ShapeDtypeStruct((n, D_FF), F32)],
        compiler_params=_cparams(("arbitrary",)),
        name="ffn_sample",
    )(*args)


def _qkv_sample_body(x_ref, g_ref, w_ref, qg_ref, kg_ref, bd_ref, q_ref, kv_ref):
    h = _rms(x_ref[...], g_ref[...]).astype(BF)
    bd = bd_ref[...]
    for g in range(N_GROUPS):
        c0 = g * 3 * ATTN_WIDTH
        q = _dot(h, w_ref[:, c0:c0 + ATTN_WIDTH])
        k = _dot(h, w_ref[:, c0 + ATTN_WIDTH:c0 + 2 * ATTN_WIDTH])
        v = _dot(h, w_ref[:, c0 + 2 * ATTN_WIDTH:c0 + 3 * ATTN_WIDTH])
        q_ref[:, g * ATTN_WIDTH:(g + 1) * ATTN_WIDTH] = _head_rms(q, qg_ref[...], bd)
        kv_ref[:, 2 * g * ATTN_WIDTH:(2 * g + 1) * ATTN_WIDTH] = _head_rms(k, kg_ref[...], bd)
        kv_ref[:, (2 * g + 1) * ATTN_WIDTH:(2 * g + 2) * ATTN_WIDTH] = v


def _qkv_sample(x, layer, attn_layer, g, w_qkv, q_gain, k_gain, bd):
    n = x.shape[0]
    out_cols = (N_GROUPS * ATTN_WIDTH, N_GROUPS * 2 * ATTN_WIDTH)
    return pl.pallas_call(
        _qkv_sample_body,
        grid=(1,),
        in_specs=[_const_spec(x.shape), _layer_spec(g, layer),
                  *[_layer_spec(a, attn_layer) for a in (w_qkv, q_gain, k_gain)], _const_spec(bd.shape)],
        out_specs=[_whole_spec((n, c)) for c in out_cols],
        out_shape=[jax.ShapeDtypeStruct((n, c), F32) for c in out_cols],
        compiler_params=_cparams(("arbitrary",)),
        name="qkv_sample",
    )(x, g, w_qkv, q_gain, k_gain, bd)


def _t5_bucket(dist):
    exact = N_BUCKETS // 2
    n = np.asarray(dist, dtype=np.float32)
    large = exact + np.log(np.maximum(n, 1.0) / exact) / math.log(MAX_DISTANCE / exact) * (N_BUCKETS - exact)
    large = np.minimum(np.floor(large), N_BUCKETS - 1)
    return np.where(n < exact, n, large).astype(np.int32)


def _bias_tables(rel_bias):
    ci = np.arange(2 * QBLK)[None, None, :]
    period = 3 * QBLK
    tabs, bts, b0s = [], [], []
    for g, (win, dil) in enumerate(GROUPS):
        bucket = _t5_bucket(dil * np.arange(N_KEYS + 1))
        bias = jnp.take(rel_bias, bucket, axis=0)[:, g * N_HEADS:(g + 1) * N_HEADS].T
        bias = bias * LOG2E
        w = jnp.concatenate([bias[:, ::-1], jnp.full((N_HEADS, period - N_KEYS - 1), NEG, F32)], axis=1)
        band = jnp.tile(w, (1, QBLK))[:, :QBLK * (period - 1)]
        band = band.reshape(N_HEADS, QBLK, period - 1)[:, :, :2 * QBLK]
        band_first = jnp.where(ci >= QBLK, band, NEG)
        pair_rows = (N_HEADS // 2, 2 * QBLK, 2 * QBLK)
        tabs.append(jnp.stack([band.reshape(pair_rows), band_first.reshape(pair_rows)], axis=0))
        vals = bias[:, N_KEYS:0:-1][:, :, None]
        skipped = jnp.full((N_HEADS, N_KEYS, dil - 1), NEG, F32)
        bts.append(jnp.concatenate([vals, skipped], axis=2).reshape(N_HEADS, 1, win))
        b0s.append(bias[:, 0:1][:, :, None])
    return jnp.stack(tabs), bts, jnp.stack(b0s)


def kernel(x_prompt, x_sample, state_sc_conv, cache_kv_d1, cache_kv_d4, cache_kv_d16, state_ffn_conv,
           norm_mix, norm_ffn, sc_w_in, sc_conv_w, sc_w_out, attn_w_qkv, attn_q_norm, attn_k_norm,
           attn_w_out, rel_bias, ffn_w_gate, ffn_w_up, ffn_conv_w, ffn_conv_b, ffn_w_down):
    depth = norm_mix.shape[0]
    nb = x_sample.shape[0]
    caches_t = [jnp.transpose(c, (0, 1, 3, 4, 5, 2)) for c in (cache_kv_d1, cache_kv_d4, cache_kv_d16)]
    tables, bts, b0 = _bias_tables(rel_bias)
    seg = np.arange(ATTN_WIDTH) // HEAD_DIM
    bd = jnp.asarray(seg[:, None] == seg[None, :], BF)

    mixer_w = (sc_w_in.astype(BF), sc_conv_w, sc_w_out.astype(BF))
    w_qkv_b, w_ao_b = attn_w_qkv.astype(BF), attn_w_out.astype(BF)
    ffn_w = (norm_ffn[:, None, :], ffn_w_gate.astype(BF), ffn_w_up.astype(BF), ffn_conv_w,
             ffn_conv_b[:, None, :], ffn_w_down.astype(BF))
    g_mix = norm_mix[:, None, :]
    q_gain = (jnp.tile(attn_q_norm, (1, N_HEADS)) * (SCALE * LOG2E))[:, None, :]
    k_gain = jnp.tile(attn_k_norm, (1, N_HEADS))[:, None, :]

    xp = x_prompt
    xs = x_sample[:, 0, :]
    p_sc, s_sc, p_ffn, s_ffn = [], [], [], []
    kv_rows = new_caches = None
    for i in range(depth):
        j = i // 2
        sf = state_ffn_conv[i]
        if i % 2 == 0:
            xp, hist = _mixer_prompt(xp, i, j, g_mix, *mixer_w)
            p_sc.append(hist)
            st = state_sc_conv[j]
            xs, u = _mixer_sample(xs, i, j, g_mix, *mixer_w, st[:, 0], st[:, 1])
            s_sc.append(jnp.stack([st[:, 1], u], axis=1))
            pre_p = pre_s = None
        else:
            qkv, kv_rows = _qkv_prompt(xp, i, j, g_mix, w_qkv_b, q_gain, k_gain, bd, kv_rows)
            pre_p = (_attn_prompt(qkv, tables), w_ao_b, j)

            qs, kvs = _qkv_sample(xs, i, j, g_mix, w_qkv_b, q_gain, k_gain, bd)
            qs = jnp.swapaxes(qs.reshape(nb, N_GROUPS, N_HEADS, HEAD_DIM), -1, -2)
            kvs = jnp.swapaxes(kvs.reshape(nb, N_GROUPS, 2, N_HEADS, HEAD_DIM), -1, -2)
            xp, fh, a_s, *new_caches = _ffn_prompt(
                xp, pre_p, i, *ffn_w, sample=(qs, kvs, caches_t, j, bts, b0, new_caches))
            a_s = jnp.swapaxes(a_s.reshape(nb, -1, HEAD_DIM, a_s.shape[-1]), -1, -2)
            pre_s = (a_s.reshape(nb, ATTN_WIDTH), w_ao_b, j)
        if i == 0:
            n_attn = attn_w_qkv.shape[0]
            fills = [c.shape for c in caches_t]
            fills += [(n_attn, xp.shape[0], win, 2 * ATTN_WIDTH) for win, _ in GROUPS]
            xp, fh, *zeros = _ffn_prompt(xp, pre_p, i, *ffn_w, zero_fill=fills)
            new_caches, kv_rows = zeros[:N_GROUPS], zeros[N_GROUPS:]
        elif i % 2 == 0:
            xp, fh = _ffn_prompt(xp, pre_p, i, *ffn_w)
        p_ffn.append(fh)
        xs, gate = _ffn_sample(xs, pre_s, i, *ffn_w, sf[:, 0], sf[:, 1])
        s_ffn.append(jnp.stack([sf[:, 1], gate], axis=1))

    s_kv = [jnp.transpose(c, (0, 1, 5, 2, 3, 4)) for c in new_caches]
    p_kv = [r.reshape(r.shape[:3] + (2, N_HEADS, HEAD_DIM)) for r in kv_rows]
    return (xp, xs[:, None, :],
            jnp.stack(p_sc, axis=0), p_kv[0], p_kv[1], p_kv[2], jnp.stack(p_ffn, axis=0),
            jnp.stack(s_sc, axis=0), s_kv[0], s_kv[1], s_kv[2], jnp.stack(s_ffn, axis=0))
```
